```python
import functools
import jax, jax.numpy as jnp
from jax import lax
import numpy as np

D_MODEL = 1024
BATCH = 4
SEQ = 4096
DEPTH = 1
DEC_BATCH = 128
DEC_SEQ = 1
PAST_LEN = 2048
PAGE_SIZE = 128

H_A = 8
DK_A = 128
DV_A = D_MODEL // H_A
CHUNK_A = 32
H_B = 8
HD_B = D_MODEL // H_B
MOBA_BLOCK = 256
MOBA_TOPK = 3
Q_BLOCK = 32
D_FF = 4 * D_MODEL
LN_EPS = 1e-5
RMS_EPS = 1e-6
DEEPNORM_ALPHA = (2.0 * DEPTH) ** 0.25
DEEPNORM_BETA = (8.0 * DEPTH) ** -0.25

SPLIT_WIDTHS = (H_A * DK_A, H_A * DK_A, H_A * DV_A, H_A * DV_A,
                H_B * HD_B, H_B * HD_B, H_B * HD_B, 2 * D_MODEL)
SPLIT_IDX = tuple(int(c) for c in np.cumsum(SPLIT_WIDTHS)[:-1])
D_IN = int(sum(SPLIT_WIDTHS))

kernel_name = "hgrn2_moba_gated_hybrid_step"


def layer_norm(x, g, b):
    xf = x.astype(jnp.float32)
    mu = jnp.mean(xf, axis=-1, keepdims=True)
    var = jnp.mean(jnp.square(xf - mu), axis=-1, keepdims=True)
    return ((xf - mu) * lax.rsqrt(var + LN_EPS)).astype(x.dtype) * g + b


def alibi_slopes():
    return jnp.asarray([2.0 ** (-8.0 * (h + 1) / H_B) for h in range(H_B)], jnp.float32)


def hgrn2_lower_bound(lb_param, layer):
    return jnp.cumsum(jax.nn.softmax(lb_param.astype(jnp.float32), axis=0), axis=0)[layer]


def hgrn2_chunk_scan(q, k, logf, v, s0):
    B, T = q.shape[:2]
    c = min(CHUNK_A, T)
    n = -(-T // c)
    pad = n * c - T
    pw = ((0, 0), (0, pad), (0, 0), (0, 0))
    q, k, logf, v = [jnp.pad(a, pw) for a in (q, k, logf, v)]

    def to_chunks(a):
        return a.reshape(B, n, c, *a.shape[2:]).swapaxes(0, 1)

    causal = jnp.tril(jnp.ones((c, c), bool))[None, :, :, None, None]

    def step(S, inp):
        qc, kc, lfc, vc = inp
        A = jnp.cumsum(lfc, axis=1)
        diff = A[:, :, None] - A[:, None, :]
        decay = jnp.exp(jnp.where(causal, diff, -jnp.inf))
        scores = jnp.einsum('bthd,bshd,btshd->bhts', qc, kc, decay)
        o_intra = jnp.einsum('bhts,bshv->bthv', scores, vc)
        o_inter = jnp.einsum('bthd,bhdv->bthv', qc * jnp.exp(A), S)
        A_last = A[:, -1]
        k_dec = kc * jnp.exp(A_last[:, None] - A)
        S_new = jnp.exp(A_last)[..., None] * S + jnp.einsum('bshd,bshv->bhdv', k_dec, vc)
        return S_new, o_intra + o_inter

    S, o = lax.scan(step, s0, (to_chunks(q), to_chunks(k), to_chunks(logf), to_chunks(v)))
    o = o.swapaxes(0, 1).reshape(B, n * c, H_A, DV_A)[:, :T]
    return o, S


def hgrn2_branch(qa, fa, ia, ga, s0, lb, g_norm):
    B, T, _ = qa.shape
    z = fa.astype(jnp.float32).reshape(B, T, H_A, DK_A)
    lbh = lb.reshape(H_A, DK_A)
    logf = jnp.logaddexp(jnp.log(lbh), jnp.log1p(-lbh) + jax.nn.log_sigmoid(z))
    k = (1.0 - lbh) * jax.nn.sigmoid(-z)
    q = jax.nn.silu(qa.astype(jnp.float32)).reshape(B, T, H_A, DK_A)
    v = ia.astype(jnp.float32).reshape(B, T, H_A, DV_A)
    o, S = hgrn2_chunk_scan(q, k, logf, v, s0.astype(jnp.float32))
    o = o * lax.rsqrt(jnp.mean(jnp.square(o), axis=-1, keepdims=True) + RMS_EPS) * g_norm.astype(jnp.float32)
    o = o * jax.nn.silu(ga.astype(jnp.float32).reshape(B, T, H_A, DV_A))
    return o.reshape(B, T, H_A * DV_A), S


def moba_attend_one(q, k, v, q_pos, slopes):
    L = k.shape[0]
    nb = L // MOBA_BLOCK
    ksel = min(MOBA_TOPK, nb)
    kb = k.reshape(nb, MOBA_BLOCK, H_B, HD_B).transpose(2, 0, 1, 3)
    vb = v.reshape(nb, MOBA_BLOCK, H_B, HD_B).transpose(2, 0, 1, 3)
    kmean = jnp.mean(kb.astype(jnp.float32), axis=2)
    scale = HD_B ** -0.5
    Tq = q.shape[0]
    c = min(Q_BLOCK, Tq)
    n = -(-Tq // c)
    pad = n * c - Tq
    qp = jnp.pad(q, ((0, pad), (0, 0), (0, 0))).reshape(n, c, H_B, HD_B)
    pp = jnp.pad(q_pos, (0, pad), mode='edge').reshape(n, c)
    h_ix = jnp.arange(H_B)[:, None, None]
    blk_ids = jnp.arange(nb)

    def one_block(inp):
        qc, pc = inp
        own = pc // MOBA_BLOCK
        gate = jnp.einsum('thd,hnd->htn', qc.astype(jnp.float32), kmean)
        gate = jnp.where(blk_ids[None, None, :] < own[None, :, None], gate, -jnp.inf)
        _, sel = lax.top_k(gate, ksel)
        sel_valid = jnp.arange(ksel)[None, :] < jnp.minimum(own, ksel)[:, None]
        idx = jnp.concatenate([sel, jnp.broadcast_to(own[None, :, None], (H_B, c, 1))], axis=-1)
        valid = jnp.concatenate([jnp.broadcast_to(sel_valid[None], (H_B, c, ksel)),
                                 jnp.ones((H_B, c, 1), bool)], axis=-1)
        kg = kb[h_ix, idx]
        vg = vb[h_ix, idx]
        kpos = idx[..., None] * MOBA_BLOCK + jnp.arange(MOBA_BLOCK)
        s = jnp.einsum('thd,htjsd->htjs', qc, kg).astype(jnp.float32) * scale
        dist = (pc[None, :, None, None] - kpos).astype(jnp.float32)
        s = s - slopes[:, None, None, None] * dist
        mask = valid[..., None] & (kpos <= pc[None, :, None, None])
        s = jnp.where(mask, s, -jnp.inf)
        p = jax.nn.softmax(s.reshape(H_B, c, -1), axis=-1).reshape(s.shape)
        return jnp.einsum('htjs,htjsd->thd', p.astype(vg.dtype), vg)

    o = lax.map(one_block, (qp, pp))
    return o.reshape(n * c, H_B, HD_B)[:Tq]


def moba_prompt(q, k, v, slopes):
    T = q.shape[1]
    L = -(-T // MOBA_BLOCK) * MOBA_BLOCK
    pw = ((0, 0), (0, L - T), (0, 0), (0, 0))
    kp, vp = jnp.pad(k, pw), jnp.pad(v, pw)
    pos = jnp.arange(T, dtype=jnp.int32)
    return lax.map(lambda a: moba_attend_one(a[0], a[1], a[2], pos, slopes), (q, kp, vp))


def moba_sample(q, k, v, slopes, cache_k, cache_v, page_table):
    T = q.shape[1]
    past = page_table.shape[1] * PAGE_SIZE
    L = -(-(past + T) // MOBA_BLOCK) * MOBA_BLOCK
    pos = past + jnp.arange(T, dtype=jnp.int32)

    def one(a):
        qb, kn, vn, pt = a
        k_all = jnp.concatenate([cache_k[pt].reshape(past, H_B, HD_B), kn], axis=0)
        v_all = jnp.concatenate([cache_v[pt].reshape(past, H_B, HD_B), vn], axis=0)
        pw = ((0, L - past - T), (0, 0), (0, 0))
        return moba_attend_one(qb, jnp.pad(k_all, pw), jnp.pad(v_all, pw), pos, slopes)

    return lax.map(one, (q, k, v, page_table))


def hybrid_layer(x, s0, attend, w_in, lb, g_norm_a, w_proj_a, w_proj_b, w_out,
                 ln1_g, ln1_b, w_up, w_down, ln2_g, ln2_b):
    B, T, _ = x.shape
    proj = x @ w_in
    qa, fa, ia, ga, qb, kb, vb, gates = jnp.split(proj, SPLIT_IDX, axis=-1)
    o_a, s_a = hgrn2_branch(qa, fa, ia, ga, s0, lb, g_norm_a)
    k_rows = kb.reshape(B, T, H_B, HD_B)
    v_rows = vb.reshape(B, T, H_B, HD_B)
    o_b = attend(qb.reshape(B, T, H_B, HD_B), k_rows, v_rows).reshape(B, T, H_B * HD_B)
    g_a, g_b = jnp.split(jax.nn.sigmoid(gates), 2, axis=-1)
    mix = (g_a * (o_a.astype(x.dtype) @ w_proj_a) + g_b * (o_b @ w_proj_b)) @ w_out
    h = layer_norm(DEEPNORM_ALPHA * x + mix, ln1_g, ln1_b)
    ff = jnp.square(jax.nn.relu(h @ w_up)) @ w_down
    y = layer_norm(DEEPNORM_ALPHA * h + ff, ln2_g, ln2_b)
    return y, s_a, k_rows, v_rows


def setup_inputs(seed: int = 0) -> dict:
    key = jax.random.key(seed)
    ks = jax.random.split(key, 24)
    n_pages = PAST_LEN // PAGE_SIZE
    n_pool = (5 * DEC_BATCH * n_pages) // 4

    def nrm(k, shape, s):
        return jax.random.normal(k, shape, jnp.float32) * s

    page_table = jax.random.permutation(ks[5], n_pool)[:DEC_BATCH * n_pages]
    page_table = page_table.reshape(DEC_BATCH, n_pages).astype(jnp.int32)
    return {
        "x_prompt": nrm(ks[0], (BATCH, SEQ, D_MODEL), 1.0),
        "x_sample": nrm(ks[1], (DEC_BATCH, DEC_SEQ, D_MODEL), 1.0),
        "cache_k": nrm(ks[2], (DEPTH, n_pool, PAGE_SIZE, H_B, HD_B), 1.0),
        "cache_v": nrm(ks[3], (DEPTH, n_pool, PAGE_SIZE, H_B, HD_B), 1.0),
        "state_hgrn": nrm(ks[4], (DEPTH, DEC_BATCH, H_A, DK_A, DV_A), 0.5),
        "page_table": page_table,
        "w_in": nrm(ks[6], (DEPTH, D_MODEL, D_IN), D_MODEL ** -0.5),
        "lb_param": nrm(ks[7], (DEPTH + 1, H_A * DK_A), 0.5),
        "g_norm_a": 1.0 + nrm(ks[8], (DEPTH, DV_A), 0.01),
        "w_proj_a": nrm(ks[9], (DEPTH, H_A * DV_A, D_MODEL), (H_A * DV_A) ** -0.5 * DEEPNORM_BETA),
        "w_proj_b": nrm(ks[10], (DEPTH, H_B * HD_B, D_MODEL), (H_B * HD_B) ** -0.5 * DEEPNORM_BETA),
        "w_out": nrm(ks[11], (DEPTH, D_MODEL, D_MODEL), D_MODEL ** -0.5 * DEEPNORM_BETA),
        "ln1_g": 1.0 + nrm(ks[12], (DEPTH, D_MODEL), 0.01),
        "ln1_b": nrm(ks[13], (DEPTH, D_MODEL), 0.01),
        "w_up": nrm(ks[14], (DEPTH, D_MODEL, D_FF), D_MODEL ** -0.5),
        "w_down": nrm(ks[15], (DEPTH, D_FF, D_MODEL), D_FF ** -0.5 * DEEPNORM_BETA),
        "ln2_g": 1.0 + nrm(ks[16], (DEPTH, D_MODEL), 0.01),
        "ln2_b": nrm(ks[17], (DEPTH, D_MODEL), 0.01),
    }


def reference(x_prompt, x_sample, cache_k, cache_v, state_hgrn, page_table,
              w_in, lb_param, g_norm_a, w_proj_a, w_proj_b, w_out,
              ln1_g, ln1_b, w_up, w_down, ln2_g, ln2_b):
    slopes = alibi_slopes()
    h_p, h_s = x_prompt, x_sample
    k_ps, v_ps, s_ps, k_ss, v_ss, s_ss = [], [], [], [], [], []
    for l in range(DEPTH):
        lb = hgrn2_lower_bound(lb_param, l)
        prm = (w_in[l], lb, g_norm_a[l], w_proj_a[l], w_proj_b[l], w_out[l],
               ln1_g[l], ln1_b[l], w_up[l], w_down[l], ln2_g[l], ln2_b[l])
        s0_p = jnp.zeros((h_p.shape[0], H_A, DK_A, DV_A), jnp.float32)
        attend_p = functools.partial(moba_prompt, slopes=slopes)
        h_p, s_p, k_p, v_p = hybrid_layer(h_p, s0_p, attend_p, *prm)
        attend_s = functools.partial(moba_sample, slopes=slopes, cache_k=cache_k[l],
                                     cache_v=cache_v[l], page_table=page_table)
        h_s, s_s, k_s, v_s = hybrid_layer(h_s, state_hgrn[l], attend_s, *prm)
        k_ps.append(k_p); v_ps.append(v_p); s_ps.append(s_p)
        k_ss.append(k_s); v_ss.append(v_s); s_ss.append(s_s)
    return (h_p, h_s, jnp.stack(k_ps), jnp.stack(v_ps), jnp.stack(s_ps),
            jnp.stack(k_ss), jnp.stack(v_ss), jnp.stack(s_ss))
```

```python
import functools

import numpy as np
import jax
import jax.numpy as jnp
from jax import lax
from jax.experimental import pallas as pl
from jax.experimental.pallas import tpu as pltpu

F32 = jnp.float32
BF16 = jnp.bfloat16

D_MODEL = 1024
N_HEADS = 8
HEAD_DIM = 128
N_SECTIONS = 9
SEC_Q, SEC_F, SEC_I, SEC_G, SEC_MQ, SEC_MK, SEC_MV, SEC_GA, SEC_GB = range(9)
DEPTH = 1
LN_EPS = 1e-5
RMS_EPS = 1e-6
DEEPNORM_ALPHA = (2.0 * DEPTH) ** 0.25
MOBA_BLOCK = 256
MOBA_TOPK = 3
PAGE_SIZE = 128
MASKED = -1e30
SUBLANES = 8
HGRN_CHUNK = 128
VMEM_LIMIT = 48 * 1024 * 1024

_NT = (((1,), (1,)), ((), ()))


def _dot(a, b):
    return jnp.dot(a, b, preferred_element_type=F32)


def _dot_nt(a, b, precision=None):
    return lax.dot_general(a, b, _NT, precision=precision, preferred_element_type=F32)


def _params(n_grid):
    return pltpu.CompilerParams(dimension_semantics=("arbitrary",) * n_grid,
                                vmem_limit_bytes=VMEM_LIMIT)


def _inproj_kernel(x_ref, w_ref, o_ref):
    o_ref[...] = _dot(x_ref[...].astype(BF16), w_ref[...])


def _inproj(x2d, w_bf16, tm):
    m = x2d.shape[0]
    return pl.pallas_call(
        _inproj_kernel,
        grid=(m // tm, N_SECTIONS),
        in_specs=[pl.BlockSpec((tm, D_MODEL), lambda i, j: (i, 0)),
                  pl.BlockSpec((D_MODEL, D_MODEL), lambda i, j: (0, j))],
        out_specs=pl.BlockSpec((None, tm, D_MODEL), lambda i, j: (j, i, 0)),
        out_shape=jax.ShapeDtypeStruct((N_SECTIONS, m, D_MODEL), F32),
        compiler_params=_params(2),
        name="inproj",
    )(x2d, w_bf16)


def _lower_bound(p):
    m = p[0:1]
    for r in range(1, p.shape[0]):
        m = jnp.maximum(m, p[r:r + 1])
    e = [jnp.exp(p[r:r + 1] - m) for r in range(p.shape[0])]
    tot = e[0]
    for r in range(1, p.shape[0]):
        tot = tot + e[r]
    return e[0] / tot


def _gates_from_z(z, lb):
    e = jnp.exp(-jnp.abs(z))
    r = 1.0 / (1.0 + e)
    er = e * r
    pos = z >= 0
    oml = 1.0 - lb
    f = lb + oml * jnp.where(pos, r, er)
    k = oml * jnp.where(pos, er, r)
    return f, k


def _bc8(row):
    return jnp.broadcast_to(row, (SUBLANES, row.shape[-1]))


def _cumsum8(x):
    r = lax.broadcasted_iota(jnp.int32, x.shape, 0)
    for sh in (1, 2, 4):
        x = x + jnp.where(r >= sh, pltpu.roll(x, sh, 0), 0.0)
    return x


def _hgrn_chunk(qa, z, v, ga, lb, gnorm, st):
    n_tiles = HGRN_CHUNK // SUBLANES
    f, k = _gates_from_z(z, lb)
    logf = jnp.log(f)
    q = qa * jax.nn.sigmoid(qa)

    def tiles(a):
        return [a[SUBLANES * j:SUBLANES * (j + 1)] for j in range(n_tiles)]

    qt, kt, vt = tiles(q), tiles(k), tiles(v)
    sc8 = [_cumsum8(t) for t in tiles(logf)]

    sc = list(sc8)
    levels = []
    nb = 1
    while nb < n_tiles:
        eq = [None] * n_tiles
        ek = [None] * n_tiles
        new = list(sc)
        for m in range(n_tiles // (2 * nb)):
            lo = range(2 * m * nb, (2 * m + 1) * nb)
            up = range((2 * m + 1) * nb, (2 * m + 2) * nb)
            tot = _bc8(sc[lo[-1]][SUBLANES - 1:SUBLANES])
            for j in lo:
                ek[j] = tot - sc[j]
            for j in up:
                eq[j] = sc[j]
                new[j] = sc[j] + tot
        levels.append((nb, eq, ek))
        sc = new
        nb *= 2
    a_t = sc
    a_last = a_t[n_tiles - 1][SUBLANES - 1:SUBLANES]

    zero = jnp.zeros((SUBLANES, HEAD_DIM), F32)
    row = lax.broadcasted_iota(jnp.int32, (HGRN_CHUNK, HGRN_CHUNK), 0)
    col = lax.broadcasted_iota(jnp.int32, (HGRN_CHUNK, HGRN_CHUNK), 1)
    scores = None
    for nb, eq, ek in levels:
        ql = jnp.concatenate([zero if eq[j] is None else qt[j] * jnp.exp(eq[j]) for j in range(n_tiles)], axis=0)
        kl = jnp.concatenate([zero if ek[j] is None else kt[j] * jnp.exp(ek[j]) for j in range(n_tiles)], axis=0)
        x = _dot_nt(ql.astype(BF16), kl.astype(BF16))
        group = 2 * nb * SUBLANES
        if group < HGRN_CHUNK:
            sh = group.bit_length() - 1
            x = jnp.where(lax.shift_right_logical(row, sh) == lax.shift_right_logical(col, sh), x, 0.0)
        scores = x if scores is None else scores + x

    r8 = lax.broadcasted_iota(jnp.int32, (SUBLANES, HEAD_DIM), 0)
    o_diag = []
    for j in range(n_tiles):
        a = sc8[j]
        acc = zero
        for s in range(SUBLANES):
            w = jnp.exp(jnp.where(r8 >= s, a - _bc8(a[s:s + 1]), -jnp.inf))
            p = w * (qt[j] * _bc8(kt[j][s:s + 1]))
            acc = acc + jnp.sum(p, axis=-1, keepdims=True) * _bc8(vt[j][s:s + 1])
        o_diag.append(acc)
    o_diag = jnp.concatenate(o_diag, axis=0)

    qg = jnp.concatenate([qt[j] * jnp.exp(a_t[j]) for j in range(n_tiles)], axis=0)
    kd = jnp.concatenate([kt[j] * jnp.exp(_bc8(a_last) - a_t[j]) for j in range(n_tiles)], axis=0)
    o = _dot_nt(qg.astype(BF16), st.astype(BF16)) + _dot(scores.astype(BF16), v.astype(BF16)) + o_diag
    st_new = st * jnp.exp(a_last) + _dot(v.T.astype(BF16), kd.astype(BF16))

    ms = jnp.mean(o * o, axis=-1, keepdims=True)
    o = o * lax.rsqrt(ms + RMS_EPS) * gnorm * (ga * jax.nn.sigmoid(ga))
    return o, st_new


def _hgrn_prompt_kernel(q_ref, f_ref, i_ref, g_ref, lbp_ref, gn_ref, o_ref, s_ref, st_scr, *, n_chunks):
    t = pl.program_id(1)

    @pl.when(t == 0)
    def _():
        st_scr[...] = jnp.zeros(st_scr.shape, F32)

    gnorm = gn_ref[...]

    def chunk_body(c, carry):
        rows = pl.ds(pl.multiple_of(c * HGRN_CHUNK, HGRN_CHUNK), HGRN_CHUNK)

        def head_body(h, carry2):
            lanes = pl.ds(pl.multiple_of(h * HEAD_DIM, HEAD_DIM), HEAD_DIM)
            lb = _lower_bound(lbp_ref[:, lanes])
            o, st_new = _hgrn_chunk(q_ref[rows, lanes], f_ref[rows, lanes], i_ref[rows, lanes],
                                    g_ref[rows, lanes], lb, gnorm, st_scr[h])
            st_scr[h] = st_new
            o_ref[rows, lanes] = o.astype(o_ref.dtype)
            return carry2

        return lax.fori_loop(0, N_HEADS, head_body, carry)

    lax.fori_loop(0, n_chunks, chunk_body, 0)

    @pl.when(t == pl.num_programs(1) - 1)
    def _():
        for h in range(N_HEADS):
            s_ref[h] = st_scr[h].T


def _hgrn_prompt(proj, lb_param, g_norm, batch, seq, tt):
    n_t = seq // tt

    def sec(s):
        return pl.BlockSpec((None, tt, D_MODEL), lambda b, t, s=s: (s, b * n_t + t, 0))

    return pl.pallas_call(
        functools.partial(_hgrn_prompt_kernel, n_chunks=tt // HGRN_CHUNK),
        grid=(batch, n_t),
        in_specs=[sec(SEC_Q), sec(SEC_F), sec(SEC_I), sec(SEC_G),
                  pl.BlockSpec(lb_param.shape, lambda b, t: (0, 0)),
                  pl.BlockSpec(g_norm.shape, lambda b, t: (0, 0))],
        out_specs=[pl.BlockSpec((tt, D_MODEL), lambda b, t: (b * n_t + t, 0)),
                   pl.BlockSpec((None, N_HEADS, HEAD_DIM, HEAD_DIM), lambda b, t: (b, 0, 0, 0))],
        out_shape=[jax.ShapeDtypeStruct((batch * seq, D_MODEL), BF16),
                   jax.ShapeDtypeStruct((batch, N_HEADS, HEAD_DIM, HEAD_DIM), F32)],
        scratch_shapes=[pltpu.VMEM((N_HEADS, HEAD_DIM, HEAD_DIM), F32)],
        compiler_params=_params(2),
        name="hgrn_prompt",
    )(proj, proj, proj, proj, lb_param, g_norm)


def _hgrn_sample_kernel(q_ref, f_ref, i_ref, g_ref, lbp_ref, gn_ref, s0_ref, o_ref, s_ref):
    sb = q_ref.shape[0]
    lb = _lower_bound(lbp_ref[...])
    f, k = _gates_from_z(f_ref[...], lb)
    qa = q_ref[...]
    q = qa * jax.nn.sigmoid(qa)
    ga = g_ref[...]
    gate = ga * jax.nn.sigmoid(ga)
    v = i_ref[...]
    gnorm = gn_ref[...]
    for b in range(sb):
        for h in range(N_HEADS):
            hl = slice(h * HEAD_DIM, (h + 1) * HEAD_DIM)
            f_col = f[b:b + 1, hl].T
            k_col = k[b:b + 1, hl].T
            q_col = q[b:b + 1, hl].T
            s_new = s0_ref[b, h] * f_col + k_col * v[b:b + 1, hl]
            s_ref[b, h] = s_new
            o = jnp.sum(s_new * q_col, axis=0, keepdims=True)
            ms = jnp.mean(o * o, axis=-1, keepdims=True)
            o_ref[b:b + 1, hl] = o * lax.rsqrt(ms + RMS_EPS) * gnorm * gate[b:b + 1, hl]


def _hgrn_sample(proj, lb_param, g_norm, state, sb):
    n = state.shape[0]

    def sec(s):
        return pl.BlockSpec((None, sb, D_MODEL), lambda i, s=s: (s, i, 0))

    st_spec = pl.BlockSpec((sb, N_HEADS, HEAD_DIM, HEAD_DIM), lambda i: (i, 0, 0, 0))
    return pl.pallas_call(
        _hgrn_sample_kernel,
        grid=(n // sb,),
        in_specs=[sec(SEC_Q), sec(SEC_F), sec(SEC_I), sec(SEC_G),
                  pl.BlockSpec(lb_param.shape, lambda i: (0, 0)),
                  pl.BlockSpec(g_norm.shape, lambda i: (0, 0)),
                  st_spec],
        out_specs=[pl.BlockSpec((sb, D_MODEL), lambda i: (i, 0)), st_spec],
        out_shape=[jax.ShapeDtypeStruct((n, D_MODEL), F32),
                   jax.ShapeDtypeStruct(state.shape, F32)],
        compiler_params=_params(1),
        name="hgrn_sample",
    )(proj, proj, proj, proj, lb_param, g_norm, state)


def _alibi_slopes():
    return np.asarray([2.0 ** (-8.0 * (h + 1) / N_HEADS) for h in range(N_HEADS)], np.float32)


def _topk_rank(g, idx, n):
    axis, count = n
    rank = jnp.zeros(g.shape, F32)
    for m in range(count):
        gm = lax.slice_in_dim(g, m, m + 1, axis=axis)
        rank = rank + ((gm > g) | ((gm == g) & (idx > m))).astype(F32)
    return rank


def _moba_prompt_kernel(slopes_ref, q_ref, k_ref, v_ref, o_ref, kb, vt, kmean, bias, *, n_blk):
    h = pl.program_id(1)
    qi = pl.program_id(2)
    blk = MOBA_BLOCK

    @pl.when(qi == 0)
    def _():
        def prep(n, carry):
            rows = pl.ds(pl.multiple_of(n * blk, blk), blk)
            kk = k_ref[rows, :]
            kb[rows, :] = kk.astype(BF16)
            kmean[pl.ds(n, 1), :] = jnp.sum(kk, axis=0, keepdims=True) * (1.0 / blk)
            vt[:, rows] = v_ref[rows, :].T.astype(BF16)
            return carry

        lax.fori_loop(0, n_blk, prep, 0)

    slope = slopes_ref[h]
    scale = HEAD_DIM ** -0.5
    q_t = q_ref[...].T
    gate = jnp.dot(kmean[...], q_t, precision=lax.Precision.HIGHEST, preferred_element_type=F32)
    nio = lax.broadcasted_iota(jnp.int32, gate.shape, 0)
    past = nio < qi
    g = jnp.where(past, gate, -jnp.inf)
    sel = past & (_topk_rank(g, nio, (0, n_blk)) < MOBA_TOPK)
    bias[...] = jnp.where(sel, 0.0, MASKED)

    q_tb = q_t.astype(BF16)
    key = lax.broadcasted_iota(jnp.int32, (blk, blk), 0)
    qry = lax.broadcasted_iota(jnp.int32, (blk, blk), 1)
    dist0 = (qry - key).astype(F32)

    own = pl.ds(pl.multiple_of(qi * blk, blk), blk)
    s = _dot(kb[own, :], q_tb) * scale - slope * dist0
    s = jnp.where(key <= qry, s, MASKED)
    m0 = jnp.max(s, axis=0, keepdims=True)
    p = jnp.exp(s - m0)
    l0 = jnp.sum(p, axis=0, keepdims=True)
    acc0 = _dot(vt[:, own], p.astype(BF16))

    def body(n, carry):
        m, l, acc = carry
        rows = pl.ds(pl.multiple_of(n * blk, blk), blk)
        off = ((qi - n) * blk).astype(F32)
        s = _dot(kb[rows, :], q_tb) * scale - slope * (dist0 + off) + bias[pl.ds(n, 1), :]
        m_new = jnp.maximum(m, jnp.max(s, axis=0, keepdims=True))
        alpha = jnp.exp(m - m_new)
        p = jnp.exp(s - m_new)
        l = alpha * l + jnp.sum(p, axis=0, keepdims=True)
        acc = acc * alpha + _dot(vt[:, rows], p.astype(BF16))
        return m_new, l, acc

    _, l, acc = lax.fori_loop(0, qi, body, (m0, l0, acc0))
    o_ref[...] = (acc * (1.0 / l)).T.astype(o_ref.dtype)


def _moba_prompt(proj, batch, seq):
    n_blk = seq // MOBA_BLOCK
    kv_spec = lambda s: pl.BlockSpec((None, seq, HEAD_DIM), lambda b, h, i, s=s: (s, b, h))
    return pl.pallas_call(
        functools.partial(_moba_prompt_kernel, n_blk=n_blk),
        grid=(batch, N_HEADS, n_blk),
        in_specs=[pl.BlockSpec(memory_space=pltpu.SMEM),
                  pl.BlockSpec((None, MOBA_BLOCK, HEAD_DIM), lambda b, h, i: (SEC_MQ, b * n_blk + i, h)),
                  kv_spec(SEC_MK), kv_spec(SEC_MV)],
        out_specs=pl.BlockSpec((MOBA_BLOCK, HEAD_DIM), lambda b, h, i: (b * n_blk + i, h)),
        out_shape=jax.ShapeDtypeStruct((batch * seq, D_MODEL), BF16),
        scratch_shapes=[pltpu.VMEM((seq, HEAD_DIM), BF16),
                        pltpu.VMEM((HEAD_DIM, seq), BF16),
                        pltpu.VMEM((n_blk, HEAD_DIM), F32),
                        pltpu.VMEM((n_blk, MOBA_BLOCK), F32)],
        compiler_params=_params(3),
        name="moba_prompt",
    )(jnp.asarray(_alibi_slopes()), proj, proj, proj)


def _moba_sample_kernel(pt_ref, slopes_ref, q_ref, kn_ref, vn_ref, kc_ref, vc_ref, o_ref,
                        sc, ksum, p_scr, acc, linv, *, n_pages):
    del pt_ref
    j = pl.program_id(1)
    past_len = n_pages * PAGE_SIZE
    n_blk = past_len // MOBA_BLOCK
    pages_per_blk = MOBA_BLOCK // PAGE_SIZE
    scale = HEAD_DIM ** -0.5
    q = q_ref[...]
    head_of_lane = lax.shift_right_logical(
        lax.broadcasted_iota(jnp.int32, (N_HEADS, D_MODEL), 1), HEAD_DIM.bit_length() - 1)
    head_of_row = lax.broadcasted_iota(jnp.int32, (N_HEADS, D_MODEL), 0)
    own_lanes = head_of_lane == head_of_row
    qr = jnp.where(own_lanes, jnp.broadcast_to(q, (N_HEADS, D_MODEL)), 0.0)

    @pl.when(j < n_pages)
    def _():
        kp = kc_ref[...]
        sc[:, pl.ds(pl.multiple_of(j * PAGE_SIZE, PAGE_SIZE), PAGE_SIZE)] = _dot_nt(
            qr.astype(BF16), kp.astype(BF16))
        cs = jnp.sum(kp, axis=0, keepdims=True)
        n = j // pages_per_blk

        @pl.when(j % pages_per_blk == 0)
        def _():
            ksum[pl.ds(n, 1), :] = cs

        @pl.when(j % pages_per_blk != 0)
        def _():
            ksum[pl.ds(n, 1), :] += cs

    @pl.when(j == n_pages - 1)
    def _():
        kmean = ksum[...] * (1.0 / MOBA_BLOCK)
        gate = _dot_nt(qr, kmean, precision=lax.Precision.HIGHEST)
        nio = lax.broadcasted_iota(jnp.int32, gate.shape, 1)
        sel = _topk_rank(gate, nio, (1, n_blk)) < MOBA_TOPK
        kpos = lax.broadcasted_iota(jnp.int32, (N_HEADS, past_len), 1)
        blk_of_key = lax.shift_right_logical(kpos, MOBA_BLOCK.bit_length() - 1)
        chosen = jnp.zeros((N_HEADS, past_len), jnp.bool_)
        for n in range(n_blk):
            chosen = chosen | ((blk_of_key == n) & sel[:, n:n + 1])
        slope = slopes_ref[...]
        s_all = sc[...] * scale - slope * (past_len - kpos).astype(F32)
        s_all = jnp.where(chosen, s_all, MASKED)
        s_self = jnp.sum(qr * kn_ref[...], axis=-1, keepdims=True) * scale
        m = jnp.maximum(jnp.max(s_all, axis=-1, keepdims=True), s_self)
        p = jnp.exp(s_all - m)
        p_self = jnp.exp(s_self - m)
        p_scr[...] = p
        linv[...] = jnp.broadcast_to(1.0 / (jnp.sum(p, axis=-1, keepdims=True) + p_self), linv.shape)
        acc[...] = p_self * jnp.broadcast_to(vn_ref[...], (N_HEADS, D_MODEL))

    @pl.when(j >= n_pages)
    def _():
        jj = j - n_pages
        pj = p_scr[:, pl.ds(pl.multiple_of(jj * PAGE_SIZE, PAGE_SIZE), PAGE_SIZE)]
        acc[...] += _dot(pj.astype(BF16), vc_ref[...].astype(BF16))

    @pl.when(j == 2 * n_pages - 1)
    def _():
        a = acc[...] * linv[:, 0:1]
        for h in range(N_HEADS):
            hl = slice(h * HEAD_DIM, (h + 1) * HEAD_DIM)
            o_ref[:, hl] = a[h:h + 1, hl]


def _moba_sample(proj3, cache_k, cache_v, page_table):
    n_seq, n_pages = page_table.shape
    past_len = n_pages * PAGE_SIZE
    assert past_len % MOBA_BLOCK == 0, "the decode token must start a MoBA block"
    n_blk = past_len // MOBA_BLOCK
    row = lambda s: pl.BlockSpec((None, None, 1, D_MODEL), lambda i, j, pt, s=s: (s, i, 0, 0))
    grid_spec = pltpu.PrefetchScalarGridSpec(
        num_scalar_prefetch=1,
        grid=(n_seq, 2 * n_pages),
        in_specs=[pl.BlockSpec((N_HEADS, 1), lambda i, j, pt: (0, 0)),
                  row(SEC_MQ), row(SEC_MK), row(SEC_MV),
                  pl.BlockSpec((None, PAGE_SIZE, D_MODEL),
                               lambda i, j, pt: (pt[i * n_pages + jnp.minimum(j, n_pages - 1)], 0, 0)),
                  pl.BlockSpec((None, PAGE_SIZE, D_MODEL),
                               lambda i, j, pt: (pt[i * n_pages + jnp.maximum(j - n_pages, 0)], 0, 0))],
        out_specs=pl.BlockSpec((None, 1, D_MODEL), lambda i, j, pt: (i, 0, 0)),
        scratch_shapes=[pltpu.VMEM((N_HEADS, past_len), F32),
                        pltpu.VMEM((n_blk, D_MODEL), F32),
                        pltpu.VMEM((N_HEADS, past_len), F32),
                        pltpu.VMEM((N_HEADS, D_MODEL), F32),
                        pltpu.VMEM((N_HEADS, HEAD_DIM), F32)])
    return pl.pallas_call(
        functools.partial(_moba_sample_kernel, n_pages=n_pages),
        grid_spec=grid_spec,
        out_shape=jax.ShapeDtypeStruct((n_seq, 1, D_MODEL), F32),
        compiler_params=_params(2),
        name="moba_sample",
    )(page_table.reshape(-1), jnp.asarray(_alibi_slopes()).reshape(N_HEADS, 1),
      proj3, proj3, proj3, cache_k, cache_v)


def _layer_norm(x, g, b):
    mu = jnp.mean(x, axis=-1, keepdims=True)
    xc = x - mu
    var = jnp.mean(xc * xc, axis=-1, keepdims=True)
    return xc * lax.rsqrt(var + LN_EPS) * g + b


def _mix_kernel(oa_ref, ob_ref, ga_ref, gb_ref, x_ref, wa_ref, wb_ref, wo_ref, g_ref, b_ref, h_ref):
    ya = _dot(oa_ref[...].astype(BF16), wa_ref[...])
    yb = _dot(ob_ref[...].astype(BF16), wb_ref[...])
    mixed = jax.nn.sigmoid(ga_ref[...]) * ya + jax.nn.sigmoid(gb_ref[...]) * yb
    mix = _dot(mixed.astype(BF16), wo_ref[...])
    h_ref[...] = _layer_norm(DEEPNORM_ALPHA * x_ref[...] + mix, g_ref[...], b_ref[...])


def _mix(o_a, o_b, proj, x2d, wa, wb, wo, ln_g, ln_b, tm):
    m = x2d.shape[0]
    rows = pl.BlockSpec((tm, D_MODEL), lambda i: (i, 0))
    sec = lambda s: pl.BlockSpec((None, tm, D_MODEL), lambda i, s=s: (s, i, 0))
    full = lambda a: pl.BlockSpec(a.shape, lambda i: (0, 0))
    return pl.pallas_call(
        _mix_kernel,
        grid=(m // tm,),
        in_specs=[rows, rows, sec(SEC_GA), sec(SEC_GB), rows, full(wa), full(wb), full(wo), full(ln_g), full(ln_b)],
        out_specs=rows,
        out_shape=jax.ShapeDtypeStruct((m, D_MODEL), F32),
        compiler_params=_params(1),
        name="mix",
    )(o_a, o_b, proj, proj, x2d, wa, wb, wo, ln_g, ln_b)


def _mlp_kernel(h_ref, wu_ref, wd_ref, g_ref, b_ref, y_ref, acc):
    j = pl.program_id(1)
    h = h_ref[...]
    u = jnp.maximum(_dot(h.astype(BF16), wu_ref[...]), 0.0)
    part = _dot((u * u).astype(BF16), wd_ref[...])

    @pl.when(j == 0)
    def _():
        acc[...] = part

    @pl.when(j > 0)
    def _():
        acc[...] += part

    @pl.when(j == pl.num_programs(1) - 1)
    def _():
        y_ref[...] = _layer_norm(DEEPNORM_ALPHA * h + acc[...], g_ref[...], b_ref[...])


def _mlp(h, wu, wd, ln_g, ln_b, tm, tf):
    m = h.shape[0]
    d_ff = wu.shape[1]
    rows = pl.BlockSpec((tm, D_MODEL), lambda i, j: (i, 0))
    full = lambda a: pl.BlockSpec(a.shape, lambda i, j: (0, 0))
    return pl.pallas_call(
        _mlp_kernel,
        grid=(m // tm, d_ff // tf),
        in_specs=[rows,
                  pl.BlockSpec((D_MODEL, tf), lambda i, j: (0, j)),
                  pl.BlockSpec((tf, D_MODEL), lambda i, j: (j, 0)),
                  full(ln_g), full(ln_b)],
        out_specs=rows,
        out_shape=jax.ShapeDtypeStruct((m, D_MODEL), F32),
        scratch_shapes=[pltpu.VMEM((tm, D_MODEL), F32)],
        compiler_params=_params(2),
        name="mlp",
    )(h, wu, wd, ln_g, ln_b)


def kernel(x_prompt, x_sample, cache_k, cache_v, state_hgrn, page_table, w_in, lb_param, g_norm_a,
           w_proj_a, w_proj_b, w_out, ln1_g, ln1_b, w_up, w_down, ln2_g, ln2_b):
    assert w_in.shape[0] == DEPTH
    batch, seq, _ = x_prompt.shape
    n_seq, dec_seq, _ = x_sample.shape
    assert dec_seq == 1
    n_pool = cache_k.shape[1]

    w_in_b = w_in[0].astype(BF16)
    wa, wb, wo = w_proj_a[0].astype(BF16), w_proj_b[0].astype(BF16), w_out[0].astype(BF16)
    wu, wd = w_up[0].astype(BF16), w_down[0].astype(BF16)
    gn = g_norm_a[0].reshape(1, HEAD_DIM)
    g1, b1 = ln1_g[0].reshape(1, D_MODEL), ln1_b[0].reshape(1, D_MODEL)
    g2, b2 = ln2_g[0].reshape(1, D_MODEL), ln2_b[0].reshape(1, D_MODEL)

    xp = x_prompt.reshape(batch * seq, D_MODEL)
    proj_p = _inproj(xp, w_in_b, tm=1024)
    oa_p, st_p = _hgrn_prompt(proj_p, lb_param, gn, batch, seq, tt=512)
    ob_p = _moba_prompt(proj_p, batch, seq)
    h_p = _mix(oa_p, ob_p, proj_p, xp, wa, wb, wo, g1, b1, tm=256)
    y_p = _mlp(h_p, wu, wd, g2, b2, tm=512, tf=1024)

    xs = x_sample.reshape(n_seq, D_MODEL)
    proj_s = _inproj(xs, w_in_b, tm=n_seq)
    oa_s, st_s = _hgrn_sample(proj_s, lb_param, gn, state_hgrn[0], sb=8)
    ob_s = _moba_sample(proj_s.reshape(N_SECTIONS, n_seq, 1, D_MODEL),
                        cache_k[0].reshape(n_pool, PAGE_SIZE, D_MODEL),
                        cache_v[0].reshape(n_pool, PAGE_SIZE, D_MODEL), page_table)
    h_s = _mix(oa_s, ob_s.reshape(n_seq, D_MODEL), proj_s, xs, wa, wb, wo, g1, b1, tm=n_seq)
    y_s = _mlp(h_s, wu, wd, g2, b2, tm=n_seq, tf=1024)

    kv_p = (DEPTH, batch, seq, N_HEADS, HEAD_DIM)
    kv_s = (DEPTH, n_seq, dec_seq, N_HEADS, HEAD_DIM)
    return (y_p.reshape(batch, seq, D_MODEL), y_s.reshape(n_seq, dec_seq, D_MODEL),
            proj_p[SEC_MK].reshape(kv_p), proj_p[SEC_MV].reshape(kv_p), st_p[None],
            proj_s[SEC_MK].reshape(kv_s), proj_s[SEC_MV].reshape(kv_s), st_s[None])
```

```python
import functools

import numpy as np
import jax
import jax.numpy as jnp
from jax import lax
from jax.experimental import pallas as pl
from jax.experimental.pallas import tpu as pltpu

F32 = jnp.float32
BF16 = jnp.bfloat16

D_MODEL = 1024
N_HEADS = 8
HEAD_DIM = 128
N_COLS = 9
COL_MK, COL_MV = 5, 6
N_SECTIONS = 7
SEC_Q, SEC_F, SEC_I, SEC_G, SEC_MQ, SEC_GA, SEC_GB = range(7)
DEPTH = 1
LN_EPS = 1e-5
RMS_EPS = 1e-6
DEEPNORM_ALPHA = (2.0 * DEPTH) ** 0.25
MOBA_BLOCK = 256
MOBA_TOPK = 3
PAGE_SIZE = 128
MASKED = -1e30
LOG2E = 1.4426950408889634
SUBLANES = 8
HGRN_CHUNK = 128
VMEM_LIMIT = 48 * 1024 * 1024

_NT = (((1,), (1,)), ((), ()))


def _dot(a, b):
    return jnp.dot(a, b, preferred_element_type=F32)


def _dot_nt(a, b, precision=None):
    return lax.dot_general(a, b, _NT, precision=precision, preferred_element_type=F32)


def _params(n_grid):
    return pltpu.CompilerParams(dimension_semantics=("arbitrary",) * n_grid,
                                vmem_limit_bytes=VMEM_LIMIT)


def _inproj_kernel(x_ref, w_ref, o_ref, k_ref, v_ref):
    j = pl.program_id(1)
    y = _dot(x_ref[...].astype(BF16), w_ref[...])

    @pl.when(j == COL_MK)
    def _():
        k_ref[...] = y

    @pl.when(j == COL_MV)
    def _():
        v_ref[...] = y

    @pl.when((j != COL_MK) & (j != COL_MV))
    def _():
        o_ref[...] = y


def _inproj(x2d, w_bf16, tm):
    m = x2d.shape[0]

    def section(i, j):
        return (jnp.where(j < COL_MK, j, jnp.where(j > COL_MV, j - 2, COL_MK - 1)), i, 0)

    rows = pl.BlockSpec((tm, D_MODEL), lambda i, j: (i, 0))
    return pl.pallas_call(
        _inproj_kernel,
        grid=(m // tm, N_COLS),
        in_specs=[rows, pl.BlockSpec((D_MODEL, D_MODEL), lambda i, j: (0, j))],
        out_specs=[pl.BlockSpec((None, tm, D_MODEL), section), rows, rows],
        out_shape=[jax.ShapeDtypeStruct((N_SECTIONS, m, D_MODEL), F32),
                   jax.ShapeDtypeStruct((m, D_MODEL), F32),
                   jax.ShapeDtypeStruct((m, D_MODEL), F32)],
        compiler_params=_params(2),
        name="inproj",
    )(x2d, w_bf16)


def _lower_bound(p):
    m = p[0:1]
    for r in range(1, p.shape[0]):
        m = jnp.maximum(m, p[r:r + 1])
    e = [jnp.exp(p[r:r + 1] - m) for r in range(p.shape[0])]
    tot = e[0]
    for r in range(1, p.shape[0]):
        tot = tot + e[r]
    return e[0] / tot


def _gates_from_z(z, lb):
    e = jnp.exp(-jnp.abs(z))
    r = 1.0 / (1.0 + e)
    er = e * r
    pos = z >= 0
    oml = 1.0 - lb
    f = lb + oml * jnp.where(pos, r, er)
    k = oml * jnp.where(pos, er, r)
    return f, k


def _bc8(row):
    return jnp.broadcast_to(row, (SUBLANES, row.shape[-1]))


def _cumsum8(x):
    r = lax.broadcasted_iota(jnp.int32, x.shape, 0)
    for sh in (1, 2, 4):
        x = x + jnp.where(r >= sh, pltpu.roll(x, sh, 0), 0.0)
    return x


def _hgrn_chunk(qa, z, v, ga, lb, gnorm, st):
    n_tiles = HGRN_CHUNK // SUBLANES
    f, k = _gates_from_z(z, lb)
    logf = jnp.log2(f)
    q = qa * jax.nn.sigmoid(qa)

    def tiles(a):
        return [a[SUBLANES * j:SUBLANES * (j + 1)] for j in range(n_tiles)]

    qt, kt, vt = tiles(q), tiles(k), tiles(v)
    sc8 = [_cumsum8(t) for t in tiles(logf)]

    sc = list(sc8)
    levels = []
    nb = 1
    while nb < n_tiles:
        eq = [None] * n_tiles
        ek = [None] * n_tiles
        new = list(sc)
        for m in range(n_tiles // (2 * nb)):
            lo = range(2 * m * nb, (2 * m + 1) * nb)
            up = range((2 * m + 1) * nb, (2 * m + 2) * nb)
            tot = _bc8(sc[lo[-1]][SUBLANES - 1:SUBLANES])
            for j in lo:
                ek[j] = tot - sc[j]
            for j in up:
                eq[j] = sc[j]
                new[j] = sc[j] + tot
        levels.append((nb, eq, ek))
        sc = new
        nb *= 2
    a_t = sc
    a_last = a_t[n_tiles - 1][SUBLANES - 1:SUBLANES]

    zero = jnp.zeros((SUBLANES, HEAD_DIM), F32)
    row = lax.broadcasted_iota(jnp.int32, (HGRN_CHUNK, HGRN_CHUNK), 0)
    col = lax.broadcasted_iota(jnp.int32, (HGRN_CHUNK, HGRN_CHUNK), 1)
    scores = None
    for nb, eq, ek in levels:
        ql = jnp.concatenate([zero if eq[j] is None else qt[j] * jnp.exp2(eq[j]) for j in range(n_tiles)], axis=0)
        kl = jnp.concatenate([zero if ek[j] is None else kt[j] * jnp.exp2(ek[j]) for j in range(n_tiles)], axis=0)
        x = _dot_nt(ql.astype(BF16), kl.astype(BF16))
        group = 2 * nb * SUBLANES
        if group < HGRN_CHUNK:
            sh = group.bit_length() - 1
            x = jnp.where(lax.shift_right_logical(row, sh) == lax.shift_right_logical(col, sh), x, 0.0)
        scores = x if scores is None else scores + x

    r8 = lax.broadcasted_iota(jnp.int32, (SUBLANES, HEAD_DIM), 0)
    o_diag = []
    for j in range(n_tiles):
        a = sc8[j]
        acc = zero
        for s in range(SUBLANES):
            w = jnp.exp2(jnp.where(r8 >= s, a - _bc8(a[s:s + 1]), -jnp.inf))
            p = w * (qt[j] * _bc8(kt[j][s:s + 1]))
            acc = acc + jnp.sum(p, axis=-1, keepdims=True) * _bc8(vt[j][s:s + 1])
        o_diag.append(acc)
    o_diag = jnp.concatenate(o_diag, axis=0)

    qg = jnp.concatenate([qt[j] * jnp.exp2(a_t[j]) for j in range(n_tiles)], axis=0)
    kd = jnp.concatenate([kt[j] * jnp.exp2(_bc8(a_last) - a_t[j]) for j in range(n_tiles)], axis=0)
    o = _dot_nt(qg.astype(BF16), st.astype(BF16)) + _dot(scores.astype(BF16), v.astype(BF16)) + o_diag
    st_new = st * jnp.exp2(a_last) + _dot(v.T.astype(BF16), kd.astype(BF16))

    ms = jnp.mean(o * o, axis=-1, keepdims=True)
    o = o * lax.rsqrt(ms + RMS_EPS) * gnorm * (ga * jax.nn.sigmoid(ga))
    return o, st_new


def _hgrn_prompt_kernel(q_ref, f_ref, i_ref, g_ref, lbp_ref, gn_ref, o_ref, s_ref, st_scr, *, n_chunks):
    t = pl.program_id(1)

    @pl.when(t == 0)
    def _():
        st_scr[...] = jnp.zeros(st_scr.shape, F32)

    gnorm = gn_ref[...]

    def chunk_body(c, carry):
        rows = pl.ds(pl.multiple_of(c * HGRN_CHUNK, HGRN_CHUNK), HGRN_CHUNK)

        def head_body(h, carry2):
            lanes = pl.ds(pl.multiple_of(h * HEAD_DIM, HEAD_DIM), HEAD_DIM)
            lb = _lower_bound(lbp_ref[:, lanes])
            o, st_new = _hgrn_chunk(q_ref[rows, lanes], f_ref[rows, lanes], i_ref[rows, lanes],
                                    g_ref[rows, lanes], lb, gnorm, st_scr[h])
            st_scr[h] = st_new
            o_ref[rows, lanes] = o.astype(o_ref.dtype)
            return carry2

        return lax.fori_loop(0, N_HEADS, head_body, carry)

    lax.fori_loop(0, n_chunks, chunk_body, 0)

    @pl.when(t == pl.num_programs(1) - 1)
    def _():
        for h in range(N_HEADS):
            s_ref[h] = st_scr[h].T


def _hgrn_prompt(proj, lb_param, g_norm, batch, seq, tt):
    n_t = seq // tt

    def sec(s):
        return pl.BlockSpec((None, tt, D_MODEL), lambda b, t, s=s: (s, b * n_t + t, 0))

    return pl.pallas_call(
        functools.partial(_hgrn_prompt_kernel, n_chunks=tt // HGRN_CHUNK),
        grid=(batch, n_t),
        in_specs=[sec(SEC_Q), sec(SEC_F), sec(SEC_I), sec(SEC_G),
                  pl.BlockSpec(lb_param.shape, lambda b, t: (0, 0)),
                  pl.BlockSpec(g_norm.shape, lambda b, t: (0, 0))],
        out_specs=[pl.BlockSpec((tt, D_MODEL), lambda b, t: (b * n_t + t, 0)),
                   pl.BlockSpec((None, N_HEADS, HEAD_DIM, HEAD_DIM), lambda b, t: (b, 0, 0, 0))],
        out_shape=[jax.ShapeDtypeStruct((batch * seq, D_MODEL), BF16),
                   jax.ShapeDtypeStruct((batch, N_HEADS, HEAD_DIM, HEAD_DIM), F32)],
        scratch_shapes=[pltpu.VMEM((N_HEADS, HEAD_DIM, HEAD_DIM), F32)],
        compiler_params=_params(2),
        name="hgrn_prompt",
    )(proj, proj, proj, proj, lb_param, g_norm)


def _hgrn_sample_kernel(q_ref, f_ref, i_ref, g_ref, lbp_ref, gn_ref, s0_ref, o_ref, s_ref):
    sb = q_ref.shape[0]
    lb = _lower_bound(lbp_ref[...])
    f, k = _gates_from_z(f_ref[...], lb)
    qa = q_ref[...]
    q = qa * jax.nn.sigmoid(qa)
    ga = g_ref[...]
    gate = ga * jax.nn.sigmoid(ga)
    v = i_ref[...]
    gnorm = gn_ref[...]
    for b in range(sb):
        for h in range(N_HEADS):
            hl = slice(h * HEAD_DIM, (h + 1) * HEAD_DIM)
            f_col = f[b:b + 1, hl].T
            k_col = k[b:b + 1, hl].T
            q_col = q[b:b + 1, hl].T
            s_new = s0_ref[b, h] * f_col + k_col * v[b:b + 1, hl]
            s_ref[b, h] = s_new
            o = jnp.sum(s_new * q_col, axis=0, keepdims=True)
            ms = jnp.mean(o * o, axis=-1, keepdims=True)
            o_ref[b:b + 1, hl] = o * lax.rsqrt(ms + RMS_EPS) * gnorm * gate[b:b + 1, hl]


def _hgrn_sample(proj, lb_param, g_norm, state, sb):
    n = state.shape[0]

    def sec(s):
        return pl.BlockSpec((None, sb, D_MODEL), lambda i, s=s: (s, i, 0))

    st_spec = pl.BlockSpec((sb, N_HEADS, HEAD_DIM, HEAD_DIM), lambda i: (i, 0, 0, 0))
    return pl.pallas_call(
        _hgrn_sample_kernel,
        grid=(n // sb,),
        in_specs=[sec(SEC_Q), sec(SEC_F), sec(SEC_I), sec(SEC_G),
                  pl.BlockSpec(lb_param.shape, lambda i: (0, 0)),
                  pl.BlockSpec(g_norm.shape, lambda i: (0, 0)),
                  st_spec],
        out_specs=[pl.BlockSpec((sb, D_MODEL), lambda i: (i, 0)), st_spec],
        out_shape=[jax.ShapeDtypeStruct((n, D_MODEL), F32),
                   jax.ShapeDtypeStruct(state.shape, F32)],
        compiler_params=_params(1),
        name="hgrn_sample",
    )(proj, proj, proj, proj, lb_param, g_norm, state)


def _alibi_slopes():
    return np.asarray([2.0 ** (-8.0 * (h + 1) / N_HEADS) for h in range(N_HEADS)], np.float32)


def _topk_rank(g, idx, n):
    axis, count = n
    rank = jnp.zeros(g.shape, F32)
    for m in range(count):
        gm = lax.slice_in_dim(g, m, m + 1, axis=axis)
        rank = rank + ((gm > g) | ((gm == g) & (idx > m))).astype(F32)
    return rank


def _moba_prompt_kernel(slopes_ref, q_ref, k_ref, v_ref, o_ref, kb, vt, kmean, bias, s_scr, *, n_blk):
    h = pl.program_id(1)
    qi = pl.program_id(2)
    blk = MOBA_BLOCK

    @pl.when(qi == 0)
    def _():
        def prep(n, carry):
            rows = pl.ds(pl.multiple_of(n * blk, blk), blk)
            kk = k_ref[rows, :]
            kb[rows, :] = kk.astype(BF16)
            kmean[pl.ds(n, 1), :] = jnp.sum(kk, axis=0, keepdims=True) * (1.0 / blk)
            vt[:, rows] = v_ref[rows, :].T.astype(BF16)
            return carry

        lax.fori_loop(0, n_blk, prep, 0)

    slope2 = slopes_ref[h] * LOG2E
    scale2 = HEAD_DIM ** -0.5 * LOG2E
    q_t = q_ref[...].T
    gate = jnp.dot(kmean[...], q_t, precision=lax.Precision.HIGHEST, preferred_element_type=F32)
    nio = lax.broadcasted_iota(jnp.int32, gate.shape, 0)
    past = nio < qi
    g = jnp.where(past, gate, -jnp.inf)
    sel = past & (_topk_rank(g, nio, (0, n_blk)) < MOBA_TOPK)
    blk_dist = ((qi - nio) * blk).astype(F32)
    bias[...] = jnp.where(sel, -slope2 * blk_dist, MASKED)

    q_tb = q_t.astype(BF16)
    key = lax.broadcasted_iota(jnp.int32, (blk, blk), 0)
    qry = lax.broadcasted_iota(jnp.int32, (blk, blk), 1)
    dist0 = slope2 * (qry - key).astype(F32)

    def pair_rows(i):
        return pl.ds(pl.multiple_of(i * 2 * blk, 2 * blk), 2 * blk)

    s_scr[0] = _dot(kb[pair_rows(0), :], q_tb)

    own = pl.ds(pl.multiple_of(qi * blk, blk), blk)
    s = _dot(kb[own, :], q_tb) * scale2 - dist0
    s = jnp.where(key <= qry, s, MASKED)
    m0 = jnp.max(s, axis=0, keepdims=True)
    p = jnp.exp2(s - m0)
    l0 = jnp.sum(p, axis=0, keepdims=True)
    acc0 = _dot(vt[:, own], p.astype(BF16))

    def body(i, carry):
        m, l, acc = carry
        n0 = 2 * i
        rows = pair_rows(i)
        slot = lax.rem(i, 2)
        s_scr[1 - slot] = _dot(kb[pair_rows(jnp.minimum(i + 1, n_blk // 2 - 1)), :], q_tb)
        s = s_scr[slot] * scale2
        sa = s[:blk] - dist0 + bias[pl.ds(n0, 1), :]
        sb = s[blk:] - dist0 + bias[pl.ds(n0 + 1, 1), :]
        m_new = jnp.maximum(m, jnp.maximum(jnp.max(sa, axis=0, keepdims=True), jnp.max(sb, axis=0, keepdims=True)))
        alpha = jnp.exp2(m - m_new)
        pa = jnp.exp2(sa - m_new)
        pb = jnp.exp2(sb - m_new)
        l = alpha * l + jnp.sum(pa, axis=0, keepdims=True) + jnp.sum(pb, axis=0, keepdims=True)
        p2 = jnp.concatenate([pa.astype(BF16), pb.astype(BF16)], axis=0)
        acc = acc * alpha + _dot(vt[:, rows], p2)
        return m_new, l, acc

    _, l, acc = lax.fori_loop(0, (qi + 1) // 2, body, (m0, l0, acc0))
    o_ref[...] = (acc * (1.0 / l)).T.astype(o_ref.dtype)


def _moba_prompt(proj, k_rows, v_rows, batch, seq):
    n_blk = seq // MOBA_BLOCK
    assert n_blk % 2 == 0, "past blocks are processed in pairs"
    kv_spec = pl.BlockSpec((seq, HEAD_DIM), lambda b, h, i: (b, h))
    return pl.pallas_call(
        functools.partial(_moba_prompt_kernel, n_blk=n_blk),
        grid=(batch, N_HEADS, n_blk),
        in_specs=[pl.BlockSpec(memory_space=pltpu.SMEM),
                  pl.BlockSpec((None, MOBA_BLOCK, HEAD_DIM), lambda b, h, i: (SEC_MQ, b * n_blk + i, h)),
                  kv_spec, kv_spec],
        out_specs=pl.BlockSpec((MOBA_BLOCK, HEAD_DIM), lambda b, h, i: (b * n_blk + i, h)),
        out_shape=jax.ShapeDtypeStruct((batch * seq, D_MODEL), BF16),
        scratch_shapes=[pltpu.VMEM((seq, HEAD_DIM), BF16),
                        pltpu.VMEM((HEAD_DIM, seq), BF16),
                        pltpu.VMEM((n_blk, HEAD_DIM), F32),
                        pltpu.VMEM((n_blk, MOBA_BLOCK), F32),
                        pltpu.VMEM((2, 2 * MOBA_BLOCK, MOBA_BLOCK), F32)],
        compiler_params=_params(3),
        name="moba_prompt",
    )(jnp.asarray(_alibi_slopes()), proj, k_rows, v_rows)


def _page_heads_on_lanes(page_ref):
    return jnp.concatenate([page_ref[pl.ds(h, PAGE_SIZE, stride=N_HEADS), :] for h in range(N_HEADS)], axis=1)


def _moba_sample_kernel(pt_ref, slopes_ref, q_ref, kn_ref, vn_ref, *rest, n_pages, group):
    del pt_ref
    k_refs, v_refs = rest[:group], rest[group:2 * group]
    o_ref, sc, ksum, p_scr, acc, linv = rest[2 * group:]
    j = pl.program_id(1)
    k_steps = n_pages // group
    past_len = n_pages * PAGE_SIZE
    n_blk = past_len // MOBA_BLOCK
    pages_per_blk = MOBA_BLOCK // PAGE_SIZE
    blks_per_step = group // pages_per_blk
    scale = HEAD_DIM ** -0.5
    q = q_ref[...]
    head_of_lane = lax.shift_right_logical(
        lax.broadcasted_iota(jnp.int32, (N_HEADS, D_MODEL), 1), HEAD_DIM.bit_length() - 1)
    head_of_row = lax.broadcasted_iota(jnp.int32, (N_HEADS, D_MODEL), 0)
    own_lanes = head_of_lane == head_of_row
    qr = jnp.where(own_lanes, jnp.broadcast_to(q, (N_HEADS, D_MODEL)), 0.0)

    @pl.when(j < k_steps)
    def _():
        qrb = qr.astype(BF16)
        sums = []
        for i in range(group):
            kp = _page_heads_on_lanes(k_refs[i])
            page = j * group + i
            sc[:, pl.ds(pl.multiple_of(page * PAGE_SIZE, PAGE_SIZE), PAGE_SIZE)] = _dot_nt(qrb, kp.astype(BF16))
            sums.append(jnp.sum(kp, axis=0, keepdims=True))
        for b in range(blks_per_step):
            tot = sums[b * pages_per_blk]
            for r in range(1, pages_per_blk):
                tot = tot + sums[b * pages_per_blk + r]
            ksum[pl.ds(j * blks_per_step + b, 1), :] = tot

    @pl.when(j == k_steps - 1)
    def _():
        kmean = ksum[...] * (1.0 / MOBA_BLOCK)
        gate = _dot_nt(qr, kmean, precision=lax.Precision.HIGHEST)
        nio = lax.broadcasted_iota(jnp.int32, gate.shape, 1)
        sel = _topk_rank(gate, nio, (1, n_blk)) < MOBA_TOPK
        kpos = lax.broadcasted_iota(jnp.int32, (N_HEADS, past_len), 1)
        blk_of_key = lax.shift_right_logical(kpos, MOBA_BLOCK.bit_length() - 1)
        chosen = jnp.zeros((N_HEADS, past_len), jnp.bool_)
        for n in range(n_blk):
            chosen = chosen | ((blk_of_key == n) & sel[:, n:n + 1])
        slope = slopes_ref[...]
        s_all = sc[...] * scale - slope * (past_len - kpos).astype(F32)
        s_all = jnp.where(chosen, s_all, MASKED)
        s_self = jnp.sum(qr * kn_ref[...], axis=-1, keepdims=True) * scale
        m = jnp.maximum(jnp.max(s_all, axis=-1, keepdims=True), s_self)
        p = jnp.exp(s_all - m)
        p_self = jnp.exp(s_self - m)
        p_scr[...] = p
        linv[...] = jnp.broadcast_to(1.0 / (jnp.sum(p, axis=-1, keepdims=True) + p_self), linv.shape)
        acc[...] = p_self * jnp.broadcast_to(vn_ref[...], (N_HEADS, D_MODEL))

    @pl.when(j >= k_steps)
    def _():
        part = None
        for i in range(group):
            page = (j - k_steps) * group + i
            pj = p_scr[:, pl.ds(pl.multiple_of(page * PAGE_SIZE, PAGE_SIZE), PAGE_SIZE)]
            d = _dot(pj.astype(BF16), _page_heads_on_lanes(v_refs[i]).astype(BF16))
            part = d if part is None else part + d
        acc[...] += part

    @pl.when(j == 2 * k_steps - 1)
    def _():
        a = acc[...] * linv[:, 0:1]
        for h in range(N_HEADS):
            hl = slice(h * HEAD_DIM, (h + 1) * HEAD_DIM)
            o_ref[:, hl] = a[h:h + 1, hl]


def _moba_sample(proj3, k_new, v_new, cache_k, cache_v, page_table, group):
    n_seq, n_pages = page_table.shape
    past_len = n_pages * PAGE_SIZE
    assert past_len % MOBA_BLOCK == 0, "the decode token must start a MoBA block"
    assert n_pages % group == 0 and group % (MOBA_BLOCK // PAGE_SIZE) == 0
    n_blk = past_len // MOBA_BLOCK
    k_steps = n_pages // group
    page_rows = PAGE_SIZE * N_HEADS
    new_row = pl.BlockSpec((None, 1, D_MODEL), lambda i, j, pt: (i, 0, 0))

    def k_page(g):
        return pl.BlockSpec((page_rows, HEAD_DIM), lambda i, j, pt: (
            pt[i * n_pages + jnp.minimum(j, k_steps - 1) * group + g], 0))

    def v_page(g):
        return pl.BlockSpec((page_rows, HEAD_DIM), lambda i, j, pt: (
            pt[i * n_pages + jnp.maximum(j - k_steps, 0) * group + g], 0))

    grid_spec = pltpu.PrefetchScalarGridSpec(
        num_scalar_prefetch=1,
        grid=(n_seq, 2 * k_steps),
        in_specs=[pl.BlockSpec((N_HEADS, 1), lambda i, j, pt: (0, 0)),
                  pl.BlockSpec((None, None, 1, D_MODEL), lambda i, j, pt: (SEC_MQ, i, 0, 0)), new_row, new_row]
                 + [k_page(g) for g in range(group)] + [v_page(g) for g in range(group)],
        out_specs=new_row,
        scratch_shapes=[pltpu.VMEM((N_HEADS, past_len), F32),
                        pltpu.VMEM((n_blk, D_MODEL), F32),
                        pltpu.VMEM((N_HEADS, past_len), F32),
                        pltpu.VMEM((N_HEADS, D_MODEL), F32),
                        pltpu.VMEM((N_HEADS, HEAD_DIM), F32)])
    return pl.pallas_call(
        functools.partial(_moba_sample_kernel, n_pages=n_pages, group=group),
        grid_spec=grid_spec,
        out_shape=jax.ShapeDtypeStruct((n_seq, 1, D_MODEL), F32),
        compiler_params=_params(2),
        name="moba_sample",
    )(page_table.reshape(-1), jnp.asarray(_alibi_slopes()).reshape(N_HEADS, 1),
      proj3, k_new, v_new, *([cache_k] * group), *([cache_v] * group))


def _layer_norm(x, g, b):
    mu = jnp.mean(x, axis=-1, keepdims=True)
    xc = x - mu
    var = jnp.mean(xc * xc, axis=-1, keepdims=True)
    return xc * lax.rsqrt(var + LN_EPS) * g + b


def _mix_kernel(oa_ref, ob_ref, ga_ref, gb_ref, x_ref, wa_ref, wb_ref, wo_ref, g_ref, b_ref, h_ref):
    ya = _dot(oa_ref[...].astype(BF16), wa_ref[...])
    yb = _dot(ob_ref[...].astype(BF16), wb_ref[...])
    mixed = jax.nn.sigmoid(ga_ref[...]) * ya + jax.nn.sigmoid(gb_ref[...]) * yb
    mix = _dot(mixed.astype(BF16), wo_ref[...])
    h_ref[...] = _layer_norm(DEEPNORM_ALPHA * x_ref[...] + mix, g_ref[...], b_ref[...])


def _mix(o_a, o_b, proj, x2d, wa, wb, wo, ln_g, ln_b, tm):
    m = x2d.shape[0]
    rows = pl.BlockSpec((tm, D_MODEL), lambda i: (i, 0))
    sec = lambda s: pl.BlockSpec((None, tm, D_MODEL), lambda i, s=s: (s, i, 0))
    full = lambda a: pl.BlockSpec(a.shape, lambda i: (0, 0))
    return pl.pallas_call(
        _mix_kernel,
        grid=(m // tm,),
        in_specs=[rows, rows, sec(SEC_GA), sec(SEC_GB), rows, full(wa), full(wb), full(wo), full(ln_g), full(ln_b)],
        out_specs=rows,
        out_shape=jax.ShapeDtypeStruct((m, D_MODEL), F32),
        compiler_params=_params(1),
        name="mix",
    )(o_a, o_b, proj, proj, x2d, wa, wb, wo, ln_g, ln_b)


def _mlp_kernel(h_ref, wu_ref, wd_ref, g_ref, b_ref, y_ref, acc):
    j = pl.program_id(1)
    h = h_ref[...]
    u = jnp.maximum(_dot(h.astype(BF16), wu_ref[...]), 0.0)
    part = _dot((u * u).astype(BF16), wd_ref[...])

    @pl.when(j == 0)
    def _():
        acc[...] = part

    @pl.when(j > 0)
    def _():
        acc[...] += part

    @pl.when(j == pl.num_programs(1) - 1)
    def _():
        y_ref[...] = _layer_norm(DEEPNORM_ALPHA * h + acc[...], g_ref[...], b_ref[...])


def _mlp(h, wu, wd, ln_g, ln_b, tm, tf):
    m = h.shape[0]
    d_ff = wu.shape[1]
    rows = pl.BlockSpec((tm, D_MODEL), lambda i, j: (i, 0))
    full = lambda a: pl.BlockSpec(a.shape, lambda i, j: (0, 0))
    return pl.pallas_call(
        _mlp_kernel,
        grid=(m // tm, d_ff // tf),
        in_specs=[rows,
                  pl.BlockSpec((D_MODEL, tf), lambda i, j: (0, j)),
                  pl.BlockSpec((tf, D_MODEL), lambda i, j: (j, 0)),
                  full(ln_g), full(ln_b)],
        out_specs=rows,
        out_shape=jax.ShapeDtypeStruct((m, D_MODEL), F32),
        scratch_shapes=[pltpu.VMEM((tm, D_MODEL), F32)],
        compiler_params=_params(2),
        name="mlp",
    )(h, wu, wd, ln_g, ln_b)


def kernel(x_prompt, x_sample, cache_k, cache_v, state_hgrn, page_table, w_in, lb_param, g_norm_a,
           w_proj_a, w_proj_b, w_out, ln1_g, ln1_b, w_up, w_down, ln2_g, ln2_b):
    assert w_in.shape[0] == DEPTH
    batch, seq, _ = x_prompt.shape
    n_seq, dec_seq, _ = x_sample.shape
    assert dec_seq == 1
    n_pool = cache_k.shape[1]

    w_in_b = w_in[0].astype(BF16)
    wa, wb, wo = w_proj_a[0].astype(BF16), w_proj_b[0].astype(BF16), w_out[0].astype(BF16)
    wu, wd = w_up[0].astype(BF16), w_down[0].astype(BF16)
    gn = g_norm_a[0].reshape(1, HEAD_DIM)
    g1, b1 = ln1_g[0].reshape(1, D_MODEL), ln1_b[0].reshape(1, D_MODEL)
    g2, b2 = ln2_g[0].reshape(1, D_MODEL), ln2_b[0].reshape(1, D_MODEL)

    xp = x_prompt.reshape(batch * seq, D_MODEL)
    proj_p, k_p, v_p = _inproj(xp, w_in_b, tm=1024)
    oa_p, st_p = _hgrn_prompt(proj_p, lb_param, gn, batch, seq, tt=512)
    ob_p = _moba_prompt(proj_p, k_p, v_p, batch, seq)
    h_p = _mix(oa_p, ob_p, proj_p, xp, wa, wb, wo, g1, b1, tm=256)
    y_p = _mlp(h_p, wu, wd, g2, b2, tm=512, tf=1024)

    xs = x_sample.reshape(n_seq, D_MODEL)
    proj_s, k_s, v_s = _inproj(xs, w_in_b, tm=n_seq)
    oa_s, st_s = _hgrn_sample(proj_s, lb_param, gn, state_hgrn[0], sb=8)
    ob_s = _moba_sample(proj_s.reshape(N_SECTIONS, n_seq, 1, D_MODEL),
                        k_s.reshape(n_seq, 1, D_MODEL), v_s.reshape(n_seq, 1, D_MODEL),
                        cache_k[0].reshape(n_pool * PAGE_SIZE * N_HEADS, HEAD_DIM),
                        cache_v[0].reshape(n_pool * PAGE_SIZE * N_HEADS, HEAD_DIM), page_table, group=8)
    h_s = _mix(oa_s, ob_s.reshape(n_seq, D_MODEL), proj_s, xs, wa, wb, wo, g1, b1, tm=n_seq)
    y_s = _mlp(h_s, wu, wd, g2, b2, tm=n_seq, tf=1024)

    kv_p = (DEPTH, batch, seq, N_HEADS, HEAD_DIM)
    kv_s = (DEPTH, n_seq, dec_seq, N_HEADS, HEAD_DIM)
    return (y_p.reshape(batch, seq, D_MODEL), y_s.reshape(n_seq, dec_seq, D_MODEL),
            k_p.reshape(kv_p), v_p.reshape(kv_p), st_p[None],
            k_s.reshape(kv_s), v_s.reshape(kv_s), st_s[None])
```

```python
import functools

import numpy as np
import jax
import jax.numpy as jnp
from jax import lax
from jax.experimental import pallas as pl
from jax.experimental.pallas import tpu as pltpu

F32 = jnp.float32
BF16 = jnp.bfloat16

D_MODEL = 1024
N_HEADS = 8
HEAD_DIM = 128
N_COLS = 9
COL_MK, COL_MV = 5, 6
N_SECTIONS = 7
SEC_Q, SEC_F, SEC_I, SEC_G, SEC_MQ, SEC_GA, SEC_GB = range(7)
DEPTH = 1
LN_EPS = 1e-5
RMS_EPS = 1e-6
DEEPNORM_ALPHA = (2.0 * DEPTH) ** 0.25
MOBA_BLOCK = 256
MOBA_TOPK = 3
PAGE_SIZE = 128
MASKED = -2e30
M_INIT = -1e30
LOG2E = 1.4426950408889634
SUBLANES = 8
STRIP = 128
HGRN_CHUNK = 128
VMEM_LIMIT = 48 * 1024 * 1024

_NT = (((1,), (1,)), ((), ()))


def _dot(a, b):
    return jnp.dot(a, b, preferred_element_type=F32)


def _dot_nt(a, b, precision=None):
    return lax.dot_general(a, b, _NT, precision=precision, preferred_element_type=F32)


def _params(n_grid):
    return pltpu.CompilerParams(dimension_semantics=("arbitrary",) * n_grid,
                                vmem_limit_bytes=VMEM_LIMIT)


def _inproj_kernel(x_ref, w_ref, o_ref, k_ref, v_ref):
    j = pl.program_id(1)
    y = _dot(x_ref[...].astype(BF16), w_ref[...])

    @pl.when(j == COL_MK)
    def _():
        k_ref[...] = y

    @pl.when(j == COL_MV)
    def _():
        v_ref[...] = y

    @pl.when((j != COL_MK) & (j != COL_MV))
    def _():
        o_ref[...] = y


def _inproj(x2d, w_bf16, tm):
    m = x2d.shape[0]

    def section(i, j):
        return (jnp.where(j < COL_MK, j, jnp.where(j > COL_MV, j - 2, COL_MK - 1)), i, 0)

    rows = pl.BlockSpec((tm, D_MODEL), lambda i, j: (i, 0))
    return pl.pallas_call(
        _inproj_kernel,
        grid=(m // tm, N_COLS),
        in_specs=[rows, pl.BlockSpec((D_MODEL, D_MODEL), lambda i, j: (0, j))],
        out_specs=[pl.BlockSpec((None, tm, D_MODEL), section), rows, rows],
        out_shape=[jax.ShapeDtypeStruct((N_SECTIONS, m, D_MODEL), F32),
                   jax.ShapeDtypeStruct((m, D_MODEL), F32),
                   jax.ShapeDtypeStruct((m, D_MODEL), F32)],
        compiler_params=_params(2),
        name="inproj",
    )(x2d, w_bf16)


def _lower_bound(p):
    m = p[0:1]
    for r in range(1, p.shape[0]):
        m = jnp.maximum(m, p[r:r + 1])
    e = [jnp.exp(p[r:r + 1] - m) for r in range(p.shape[0])]
    tot = e[0]
    for r in range(1, p.shape[0]):
        tot = tot + e[r]
    return e[0] / tot


def _gates_from_z(z, lb):
    e = jnp.exp(-jnp.abs(z))
    r = 1.0 / (1.0 + e)
    er = e * r
    pos = z >= 0
    oml = 1.0 - lb
    f = lb + oml * jnp.where(pos, r, er)
    k = oml * jnp.where(pos, er, r)
    return f, k


def _bc8(row):
    return jnp.broadcast_to(row, (SUBLANES, row.shape[-1]))


def _cumsum8(x):
    r = lax.broadcasted_iota(jnp.int32, x.shape, 0)
    for sh in (1, 2, 4):
        x = x + jnp.where(r >= sh, pltpu.roll(x, sh, 0), 0.0)
    return x


def _hgrn_chunk(qa, z, v, ga, lb, gnorm, st):
    n_tiles = HGRN_CHUNK // SUBLANES
    f, k = _gates_from_z(z, lb)
    logf = jnp.log2(f)
    q = qa * jax.nn.sigmoid(qa)

    def tiles(a):
        return [a[SUBLANES * j:SUBLANES * (j + 1)] for j in range(n_tiles)]

    qt, kt, vt = tiles(q), tiles(k), tiles(v)
    sc8 = [_cumsum8(t) for t in tiles(logf)]

    sc = list(sc8)
    levels = []
    nb = 1
    while nb < n_tiles:
        eq = [None] * n_tiles
        ek = [None] * n_tiles
        new = list(sc)
        for m in range(n_tiles // (2 * nb)):
            lo = range(2 * m * nb, (2 * m + 1) * nb)
            up = range((2 * m + 1) * nb, (2 * m + 2) * nb)
            tot = _bc8(sc[lo[-1]][SUBLANES - 1:SUBLANES])
            for j in lo:
                ek[j] = tot - sc[j]
            for j in up:
                eq[j] = sc[j]
                new[j] = sc[j] + tot
        levels.append((nb, eq, ek))
        sc = new
        nb *= 2
    a_t = sc
    a_last = a_t[n_tiles - 1][SUBLANES - 1:SUBLANES]

    zero = jnp.zeros((SUBLANES, HEAD_DIM), F32)
    row = lax.broadcasted_iota(jnp.int32, (HGRN_CHUNK, HGRN_CHUNK), 0)
    col = lax.broadcasted_iota(jnp.int32, (HGRN_CHUNK, HGRN_CHUNK), 1)
    scores = None
    for nb, eq, ek in levels:
        ql = jnp.concatenate([zero if eq[j] is None else qt[j] * jnp.exp2(eq[j]) for j in range(n_tiles)], axis=0)
        kl = jnp.concatenate([zero if ek[j] is None else kt[j] * jnp.exp2(ek[j]) for j in range(n_tiles)], axis=0)
        x = _dot_nt(ql.astype(BF16), kl.astype(BF16))
        group = 2 * nb * SUBLANES
        if group < HGRN_CHUNK:
            sh = group.bit_length() - 1
            x = jnp.where(lax.shift_right_logical(row, sh) == lax.shift_right_logical(col, sh), x, 0.0)
        scores = x if scores is None else scores + x

    r8 = lax.broadcasted_iota(jnp.int32, (SUBLANES, HEAD_DIM), 0)
    o_diag = []
    for j in range(n_tiles):
        a = sc8[j]
        acc = zero
        for s in range(SUBLANES):
            w = jnp.exp2(jnp.where(r8 >= s, a - _bc8(a[s:s + 1]), -jnp.inf))
            p = w * (qt[j] * _bc8(kt[j][s:s + 1]))
            acc = acc + jnp.sum(p, axis=-1, keepdims=True) * _bc8(vt[j][s:s + 1])
        o_diag.append(acc)
    o_diag = jnp.concatenate(o_diag, axis=0)

    qg = jnp.concatenate([qt[j] * jnp.exp2(a_t[j]) for j in range(n_tiles)], axis=0)
    kd = jnp.concatenate([kt[j] * jnp.exp2(_bc8(a_last) - a_t[j]) for j in range(n_tiles)], axis=0)
    o = _dot_nt(qg.astype(BF16), st.astype(BF16)) + _dot(scores.astype(BF16), v.astype(BF16)) + o_diag
    st_new = st * jnp.exp2(a_last) + _dot(v.T.astype(BF16), kd.astype(BF16))

    ms = jnp.mean(o * o, axis=-1, keepdims=True)
    o = o * lax.rsqrt(ms + RMS_EPS) * gnorm * (ga * jax.nn.sigmoid(ga))
    return o, st_new


def _hgrn_prompt_kernel(q_ref, f_ref, i_ref, g_ref, lbp_ref, gn_ref, o_ref, s_ref, st_scr, *, n_chunks):
    t = pl.program_id(1)

    @pl.when(t == 0)
    def _():
        st_scr[...] = jnp.zeros(st_scr.shape, F32)

    gnorm = gn_ref[...]

    def chunk_body(c, carry):
        rows = pl.ds(pl.multiple_of(c * HGRN_CHUNK, HGRN_CHUNK), HGRN_CHUNK)

        def head_body(h, carry2):
            lanes = pl.ds(pl.multiple_of(h * HEAD_DIM, HEAD_DIM), HEAD_DIM)
            lb = _lower_bound(lbp_ref[:, lanes])
            o, st_new = _hgrn_chunk(q_ref[rows, lanes], f_ref[rows, lanes], i_ref[rows, lanes],
                                    g_ref[rows, lanes], lb, gnorm, st_scr[h])
            st_scr[h] = st_new
            o_ref[rows, lanes] = o.astype(o_ref.dtype)
            return carry2

        return lax.fori_loop(0, N_HEADS, head_body, carry)

    lax.fori_loop(0, n_chunks, chunk_body, 0)

    @pl.when(t == pl.num_programs(1) - 1)
    def _():
        for h in range(N_HEADS):
            s_ref[h] = st_scr[h].T


def _hgrn_prompt(proj, lb_param, g_norm, batch, seq, tt):
    n_t = seq // tt

    def sec(s):
        return pl.BlockSpec((None, tt, D_MODEL), lambda b, t, s=s: (s, b * n_t + t, 0))

    return pl.pallas_call(
        functools.partial(_hgrn_prompt_kernel, n_chunks=tt // HGRN_CHUNK),
        grid=(batch, n_t),
        in_specs=[sec(SEC_Q), sec(SEC_F), sec(SEC_I), sec(SEC_G),
                  pl.BlockSpec(lb_param.shape, lambda b, t: (0, 0)),
                  pl.BlockSpec(g_norm.shape, lambda b, t: (0, 0))],
        out_specs=[pl.BlockSpec((tt, D_MODEL), lambda b, t: (b * n_t + t, 0)),
                   pl.BlockSpec((None, N_HEADS, HEAD_DIM, HEAD_DIM), lambda b, t: (b, 0, 0, 0))],
        out_shape=[jax.ShapeDtypeStruct((batch * seq, D_MODEL), BF16),
                   jax.ShapeDtypeStruct((batch, N_HEADS, HEAD_DIM, HEAD_DIM), F32)],
        scratch_shapes=[pltpu.VMEM((N_HEADS, HEAD_DIM, HEAD_DIM), F32)],
        compiler_params=_params(2),
        name="hgrn_prompt",
    )(proj, proj, proj, proj, lb_param, g_norm)


def _hgrn_sample_kernel(q_ref, f_ref, i_ref, g_ref, lbp_ref, gn_ref, s0_ref, o_ref, s_ref):
    sb = q_ref.shape[0]
    lb = _lower_bound(lbp_ref[...])
    f, k = _gates_from_z(f_ref[...], lb)
    qa = q_ref[...]
    q = qa * jax.nn.sigmoid(qa)
    ga = g_ref[...]
    gate = ga * jax.nn.sigmoid(ga)
    v = i_ref[...]
    gnorm = gn_ref[...]
    for b in range(sb):
        for h in range(N_HEADS):
            hl = slice(h * HEAD_DIM, (h + 1) * HEAD_DIM)
            f_col = f[b:b + 1, hl].T
            k_col = k[b:b + 1, hl].T
            q_col = q[b:b + 1, hl].T
            s_new = s0_ref[b, h] * f_col + k_col * v[b:b + 1, hl]
            s_ref[b, h] = s_new
            o = jnp.sum(s_new * q_col, axis=0, keepdims=True)
            ms = jnp.mean(o * o, axis=-1, keepdims=True)
            o_ref[b:b + 1, hl] = o * lax.rsqrt(ms + RMS_EPS) * gnorm * gate[b:b + 1, hl]


def _hgrn_sample(proj, lb_param, g_norm, state, sb):
    n = state.shape[0]

    def sec(s):
        return pl.BlockSpec((None, sb, D_MODEL), lambda i, s=s: (s, i, 0))

    st_spec = pl.BlockSpec((sb, N_HEADS, HEAD_DIM, HEAD_DIM), lambda i: (i, 0, 0, 0))
    return pl.pallas_call(
        _hgrn_sample_kernel,
        grid=(n // sb,),
        in_specs=[sec(SEC_Q), sec(SEC_F), sec(SEC_I), sec(SEC_G),
                  pl.BlockSpec(lb_param.shape, lambda i: (0, 0)),
                  pl.BlockSpec(g_norm.shape, lambda i: (0, 0)),
                  st_spec],
        out_specs=[pl.BlockSpec((sb, D_MODEL), lambda i: (i, 0)), st_spec],
        out_shape=[jax.ShapeDtypeStruct((n, D_MODEL), F32),
                   jax.ShapeDtypeStruct(state.shape, F32)],
        compiler_params=_params(1),
        name="hgrn_sample",
    )(proj, proj, proj, proj, lb_param, g_norm, state)


def _alibi_slopes():
    return np.asarray([2.0 ** (-8.0 * (h + 1) / N_HEADS) for h in range(N_HEADS)], np.float32)


def _topk_rank(g, idx, n):
    axis, count = n
    rank = jnp.zeros(g.shape, F32)
    for m in range(count):
        gm = lax.slice_in_dim(g, m, m + 1, axis=axis)
        rank = rank + ((gm > g) | ((gm == g) & (idx > m))).astype(F32)
    return rank


def _moba_prompt_kernel(slopes_ref, q_ref, k_ref, v_ref, o_ref,
                        kb, vt, qtb, kmean, bias, dist, s_scr, p_scr, *, n_blk):
    h = pl.program_id(1)
    blk = MOBA_BLOCK
    n_pairs = n_blk // 2
    slope2 = slopes_ref[h] * LOG2E
    scale2 = HEAD_DIM ** -0.5 * LOG2E

    def blk_rows(n):
        return pl.ds(pl.multiple_of(n * blk, blk), blk)

    def pair_rows(i):
        return pl.ds(pl.multiple_of(i * 2 * blk, 2 * blk), 2 * blk)

    def prep(n, carry):
        rows = blk_rows(n)
        kk = k_ref[rows, :]
        kb[rows, :] = kk.astype(BF16)
        kmean[pl.ds(n, 1), :] = jnp.sum(kk, axis=0, keepdims=True) * (1.0 / blk)
        vt[:, rows] = v_ref[rows, :].T.astype(BF16)
        return carry

    lax.fori_loop(0, n_blk, prep, 0)

    key = lax.broadcasted_iota(jnp.int32, (blk, blk), 0)
    qry = lax.broadcasted_iota(jnp.int32, (blk, blk), 1)
    d0 = slope2 * (qry - key).astype(F32)
    dist[0] = d0
    dist[1] = jnp.where(key <= qry, d0, -MASKED)

    def choose(qi, carry):
        rows = blk_rows(qi)
        q_t = q_ref[rows, :].T
        qtb[:, rows] = q_t.astype(BF16)
        gate = jnp.dot(kmean[...], q_t, precision=lax.Precision.HIGHEST, preferred_element_type=F32)
        nio = lax.broadcasted_iota(jnp.int32, gate.shape, 0)
        past = nio < qi
        g = jnp.where(past, gate, -jnp.inf)
        attend = (past & (_topk_rank(g, nio, (0, n_blk)) < MOBA_TOPK)) | (nio == qi)
        bias[:, rows] = jnp.where(attend, -slope2 * ((qi - nio) * blk).astype(F32), MASKED)
        return carry

    lax.fori_loop(0, n_blk, choose, 0)

    def tile(j, carry):
        cols = pair_rows(j)
        q_tb = qtb[:, cols]
        s_scr[0] = _dot(kb[pair_rows(0), :], q_tb)
        p_scr[0] = jnp.zeros(p_scr.shape[1:], BF16)

        def body(i, c):
            m, l, acc, alpha_prev = c
            slot = lax.rem(i, 2)
            own = (i == j).astype(jnp.int32)
            bias_a = bias[pl.ds(2 * i, 1), cols]
            bias_b = bias[pl.ds(2 * i + 1, 1), cols]
            strips = []
            for t in range(2 * blk // STRIP):
                lanes = slice(t * STRIP, (t + 1) * STRIP)
                in_blk = slice(t * STRIP % blk, t * STRIP % blk + STRIP)
                first = t * STRIP < blk
                sa = (s_scr[slot, :blk, lanes] * scale2 - dist[own if first else 0, :, in_blk]
                      + bias_a[:, lanes])
                sb = (s_scr[slot, blk:, lanes] * scale2 - dist[0 if first else own, :, in_blk]
                      + bias_b[:, lanes])
                m_t = jnp.maximum(m[:, lanes], jnp.maximum(jnp.max(sa, axis=0, keepdims=True),
                                                           jnp.max(sb, axis=0, keepdims=True)))
                alpha_t = jnp.exp2(m[:, lanes] - m_t)
                pa = jnp.exp2(sa - m_t)
                pb = jnp.exp2(sb - m_t)
                l_t = alpha_t * l[:, lanes] + jnp.sum(pa, axis=0, keepdims=True) + jnp.sum(pb, axis=0, keepdims=True)
                strips.append((m_t, l_t, alpha_t, pa.astype(BF16), pb.astype(BF16)))
            p_prev = p_scr[slot]
            acc = acc * alpha_prev + _dot(vt[:, pair_rows(jnp.maximum(i - 1, 0))], p_prev)
            s_scr[1 - slot] = _dot(kb[pair_rows(jnp.minimum(i + 1, n_pairs - 1)), :], q_tb)
            for t, (_, _, _, pa, pb) in enumerate(strips):
                lanes = slice(t * STRIP, (t + 1) * STRIP)
                p_scr[1 - slot, :blk, lanes] = pa
                p_scr[1 - slot, blk:, lanes] = pb
            m_new, l, alpha = (jnp.concatenate([st[k] for st in strips], axis=1) for k in range(3))
            return m_new, l, acc, alpha

        n_iter = j + 1
        init = (jnp.full((1, 2 * blk), M_INIT, F32), jnp.zeros((1, 2 * blk), F32),
                jnp.zeros((HEAD_DIM, 2 * blk), F32), jnp.ones((1, 2 * blk), F32))
        _, l, acc, alpha = lax.fori_loop(0, n_iter, body, init)
        acc = acc * alpha + _dot(vt[:, pair_rows(j)], p_scr[lax.rem(n_iter, 2)])
        o_ref[cols, :] = (acc * (1.0 / l)).T.astype(o_ref.dtype)
        return carry

    lax.fori_loop(0, n_pairs, tile, 0)


def _moba_prompt(proj, k_rows, v_rows, batch, seq):
    n_blk = seq // MOBA_BLOCK
    assert n_blk % 2 == 0, "key blocks are processed in pairs"
    head = pl.BlockSpec((seq, HEAD_DIM), lambda b, h: (b, h))
    return pl.pallas_call(
        functools.partial(_moba_prompt_kernel, n_blk=n_blk),
        grid=(batch, N_HEADS),
        in_specs=[pl.BlockSpec(memory_space=pltpu.SMEM),
                  pl.BlockSpec((None, seq, HEAD_DIM), lambda b, h: (SEC_MQ, b, h)),
                  head, head],
        out_specs=head,
        out_shape=jax.ShapeDtypeStruct((batch * seq, D_MODEL), BF16),
        scratch_shapes=[pltpu.VMEM((seq, HEAD_DIM), BF16),
                        pltpu.VMEM((HEAD_DIM, seq), BF16),
                        pltpu.VMEM((HEAD_DIM, seq), BF16),
                        pltpu.VMEM((n_blk, HEAD_DIM), F32),
                        pltpu.VMEM((n_blk, seq), F32),
                        pltpu.VMEM((2, MOBA_BLOCK, MOBA_BLOCK), F32),
                        pltpu.VMEM((2, 2 * MOBA_BLOCK, 2 * MOBA_BLOCK), F32),
                        pltpu.VMEM((2, 2 * MOBA_BLOCK, 2 * MOBA_BLOCK), BF16)],
        compiler_params=_params(2),
        name="moba_prompt",
    )(jnp.asarray(_alibi_slopes()), proj, k_rows, v_rows)


def _page_heads_on_lanes(page_ref):
    return jnp.concatenate([page_ref[pl.ds(h, PAGE_SIZE, stride=N_HEADS), :] for h in range(N_HEADS)], axis=1)


def _moba_sample_kernel(pt_ref, slopes_ref, q_ref, kn_ref, vn_ref, *rest, n_pages, group):
    del pt_ref
    k_refs, v_refs = rest[:group], rest[group:2 * group]
    o_ref, sc, ksum, p_scr, acc, linv = rest[2 * group:]
    j = pl.program_id(1)
    k_steps = n_pages // group
    past_len = n_pages * PAGE_SIZE
    n_blk = past_len // MOBA_BLOCK
    pages_per_blk = MOBA_BLOCK // PAGE_SIZE
    blks_per_step = group // pages_per_blk
    scale = HEAD_DIM ** -0.5
    q = q_ref[...]
    head_of_lane = lax.shift_right_logical(
        lax.broadcasted_iota(jnp.int32, (N_HEADS, D_MODEL), 1), HEAD_DIM.bit_length() - 1)
    head_of_row = lax.broadcasted_iota(jnp.int32, (N_HEADS, D_MODEL), 0)
    own_lanes = head_of_lane == head_of_row
    qr = jnp.where(own_lanes, jnp.broadcast_to(q, (N_HEADS, D_MODEL)), 0.0)

    @pl.when(j < k_steps)
    def _():
        qrb = qr.astype(BF16)
        sums = []
        for i in range(group):
            kp = _page_heads_on_lanes(k_refs[i])
            page = j * group + i
            sc[:, pl.ds(pl.multiple_of(page * PAGE_SIZE, PAGE_SIZE), PAGE_SIZE)] = _dot_nt(qrb, kp.astype(BF16))
            sums.append(jnp.sum(kp, axis=0, keepdims=True))
        for b in range(blks_per_step):
            tot = sums[b * pages_per_blk]
            for r in range(1, pages_per_blk):
                tot = tot + sums[b * pages_per_blk + r]
            ksum[pl.ds(j * blks_per_step + b, 1), :] = tot

    @pl.when(j == k_steps - 1)
    def _():
        kmean = ksum[...] * (1.0 / MOBA_BLOCK)
        gate = _dot_nt(qr, kmean, precision=lax.Precision.HIGHEST)
        nio = lax.broadcasted_iota(jnp.int32, gate.shape, 1)
        sel = _topk_rank(gate, nio, (1, n_blk)) < MOBA_TOPK
        kpos = lax.broadcasted_iota(jnp.int32, (N_HEADS, past_len), 1)
        blk_of_key = lax.shift_right_logical(kpos, MOBA_BLOCK.bit_length() - 1)
        chosen = jnp.zeros((N_HEADS, past_len), jnp.bool_)
        for n in range(n_blk):
            chosen = chosen | ((blk_of_key == n) & sel[:, n:n + 1])
        slope = slopes_ref[...]
        s_all = sc[...] * scale - slope * (past_len - kpos).astype(F32)
        s_all = jnp.where(chosen, s_all, MASKED)
        s_self = jnp.sum(qr * kn_ref[...], axis=-1, keepdims=True) * scale
        m = jnp.maximum(jnp.max(s_all, axis=-1, keepdims=True), s_self)
        p = jnp.exp(s_all - m)
        p_self = jnp.exp(s_self - m)
        p_scr[...] = p
        linv[...] = jnp.broadcast_to(1.0 / (jnp.sum(p, axis=-1, keepdims=True) + p_self), linv.shape)
        acc[...] = p_self * jnp.broadcast_to(vn_ref[...], (N_HEADS, D_MODEL))

    @pl.when(j >= k_steps)
    def _():
        part = None
        for i in range(group):
            page = (j - k_steps) * group + i
            pj = p_scr[:, pl.ds(pl.multiple_of(page * PAGE_SIZE, PAGE_SIZE), PAGE_SIZE)]
            d = _dot(pj.astype(BF16), _page_heads_on_lanes(v_refs[i]).astype(BF16))
            part = d if part is None else part + d
        acc[...] += part

    @pl.when(j == 2 * k_steps - 1)
    def _():
        a = acc[...] * linv[:, 0:1]
        for h in range(N_HEADS):
            hl = slice(h * HEAD_DIM, (h + 1) * HEAD_DIM)
            o_ref[:, hl] = a[h:h + 1, hl]


def _moba_sample(proj3, k_new, v_new, cache_k, cache_v, page_table, group):
    n_seq, n_pages = page_table.shape
    past_len = n_pages * PAGE_SIZE
    assert past_len % MOBA_BLOCK == 0, "the decode token must start a MoBA block"
    assert n_pages % group == 0 and group % (MOBA_BLOCK // PAGE_SIZE) == 0
    n_blk = past_len // MOBA_BLOCK
    k_steps = n_pages // group
    page_rows = PAGE_SIZE * N_HEADS
    new_row = pl.BlockSpec((None, 1, D_MODEL), lambda i, j, pt: (i, 0, 0))

    def k_page(g):
        return pl.BlockSpec((page_rows, HEAD_DIM), lambda i, j, pt: (
            pt[i * n_pages + jnp.minimum(j, k_steps - 1) * group + g], 0))

    def v_page(g):
        def index(i, j, pt):
            in_k = j < k_steps
            seq_i = jnp.maximum(i - in_k.astype(jnp.int32), 0)
            grp = jnp.where(in_k, k_steps - 1, j - k_steps)
            return pt[seq_i * n_pages + grp * group + g], 0
        return pl.BlockSpec((page_rows, HEAD_DIM), index)

    grid_spec = pltpu.PrefetchScalarGridSpec(
        num_scalar_prefetch=1,
        grid=(n_seq, 2 * k_steps),
        in_specs=[pl.BlockSpec((N_HEADS, 1), lambda i, j, pt: (0, 0)),
                  pl.BlockSpec((None, None, 1, D_MODEL), lambda i, j, pt: (SEC_MQ, i, 0, 0)), new_row, new_row]
                 + [k_page(g) for g in range(group)] + [v_page(g) for g in range(group)],
        out_specs=new_row,
        scratch_shapes=[pltpu.VMEM((N_HEADS, past_len), F32),
                        pltpu.VMEM((n_blk, D_MODEL), F32),
                        pltpu.VMEM((N_HEADS, past_len), F32),
                        pltpu.VMEM((N_HEADS, D_MODEL), F32),
                        pltpu.VMEM((N_HEADS, HEAD_DIM), F32)])
    return pl.pallas_call(
        functools.partial(_moba_sample_kernel, n_pages=n_pages, group=group),
        grid_spec=grid_spec,
        out_shape=jax.ShapeDtypeStruct((n_seq, 1, D_MODEL), F32),
        compiler_params=_params(2),
        name="moba_sample",
    )(page_table.reshape(-1), jnp.asarray(_alibi_slopes()).reshape(N_HEADS, 1),
      proj3, k_new, v_new, *([cache_k] * group), *([cache_v] * group))


def _layer_norm(x, g, b):
    mu = jnp.mean(x, axis=-1, keepdims=True)
    xc = x - mu
    var = jnp.mean(xc * xc, axis=-1, keepdims=True)
    return xc * lax.rsqrt(var + LN_EPS) * g + b


def _mix_kernel(oa_ref, ob_ref, ga_ref, gb_ref, x_ref, wa_ref, wb_ref, wo_ref, g_ref, b_ref, h_ref):
    ya = _dot(oa_ref[...].astype(BF16), wa_ref[...])
    yb = _dot(ob_ref[...].astype(BF16), wb_ref[...])
    mixed = jax.nn.sigmoid(ga_ref[...]) * ya + jax.nn.sigmoid(gb_ref[...]) * yb
    mix = _dot(mixed.astype(BF16), wo_ref[...])
    h_ref[...] = _layer_norm(DEEPNORM_ALPHA * x_ref[...] + mix, g_ref[...], b_ref[...])


def _mix(o_a, o_b, proj, x2d, wa, wb, wo, ln_g, ln_b, tm):
    m = x2d.shape[0]
    rows = pl.BlockSpec((tm, D_MODEL), lambda i: (i, 0))
    sec = lambda s: pl.BlockSpec((None, tm, D_MODEL), lambda i, s=s: (s, i, 0))
    full = lambda a: pl.BlockSpec(a.shape, lambda i: (0, 0))
    return pl.pallas_call(
        _mix_kernel,
        grid=(m // tm,),
        in_specs=[rows, rows, sec(SEC_GA), sec(SEC_GB), rows, full(wa), full(wb), full(wo), full(ln_g), full(ln_b)],
        out_specs=rows,
        out_shape=jax.ShapeDtypeStruct((m, D_MODEL), F32),
        compiler_params=_params(1),
        name="mix",
    )(o_a, o_b, proj, proj, x2d, wa, wb, wo, ln_g, ln_b)


def _mlp_kernel(h_ref, wu_ref, wd_ref, g_ref, b_ref, y_ref, acc):
    j = pl.program_id(1)
    h = h_ref[...]
    u = jnp.maximum(_dot(h.astype(BF16), wu_ref[...]), 0.0)
    part = _dot((u * u).astype(BF16), wd_ref[...])

    @pl.when(j == 0)
    def _():
        acc[...] = part

    @pl.when(j > 0)
    def _():
        acc[...] += part

    @pl.when(j == pl.num_programs(1) - 1)
    def _():
        y_ref[...] = _layer_norm(DEEPNORM_ALPHA * h + acc[...], g_ref[...], b_ref[...])


def _mlp(h, wu, wd, ln_g, ln_b, tm, tf):
    m = h.shape[0]
    d_ff = wu.shape[1]
    rows = pl.BlockSpec((tm, D_MODEL), lambda i, j: (i, 0))
    full = lambda a: pl.BlockSpec(a.shape, lambda i, j: (0, 0))
    return pl.pallas_call(
        _mlp_kernel,
        grid=(m // tm, d_ff // tf),
        in_specs=[rows,
                  pl.BlockSpec((D_MODEL, tf), lambda i, j: (0, j)),
                  pl.BlockSpec((tf, D_MODEL), lambda i, j: (j, 0)),
                  full(ln_g), full(ln_b)],
        out_specs=rows,
        out_shape=jax.ShapeDtypeStruct((m, D_MODEL), F32),
        scratch_shapes=[pltpu.VMEM((tm, D_MODEL), F32)],
        compiler_params=_params(2),
        name="mlp",
    )(h, wu, wd, ln_g, ln_b)


def kernel(x_prompt, x_sample, cache_k, cache_v, state_hgrn, page_table, w_in, lb_param, g_norm_a,
           w_proj_a, w_proj_b, w_out, ln1_g, ln1_b, w_up, w_down, ln2_g, ln2_b):
    assert w_in.shape[0] == DEPTH
    batch, seq, _ = x_prompt.shape
    n_seq, dec_seq, _ = x_sample.shape
    assert dec_seq == 1
    n_pool = cache_k.shape[1]

    w_in_b = w_in[0].astype(BF16)
    wa, wb, wo = w_proj_a[0].astype(BF16), w_proj_b[0].astype(BF16), w_out[0].astype(BF16)
    wu, wd = w_up[0].astype(BF16), w_down[0].astype(BF16)
    gn = g_norm_a[0].reshape(1, HEAD_DIM)
    g1, b1 = ln1_g[0].reshape(1, D_MODEL), ln1_b[0].reshape(1, D_MODEL)
    g2, b2 = ln2_g[0].reshape(1, D_MODEL), ln2_b[0].reshape(1, D_MODEL)

    xp = x_prompt.reshape(batch * seq, D_MODEL)
    proj_p, k_p, v_p = _inproj(xp, w_in_b, tm=1024)
    oa_p, st_p = _hgrn_prompt(proj_p, lb_param, gn, batch, seq, tt=512)
    ob_p = _moba_prompt(proj_p, k_p, v_p, batch, seq)
    h_p = _mix(oa_p, ob_p, proj_p, xp, wa, wb, wo, g1, b1, tm=256)
    y_p = _mlp(h_p, wu, wd, g2, b2, tm=512, tf=1024)

    xs = x_sample.reshape(n_seq, D_MODEL)
    proj_s, k_s, v_s = _inproj(xs, w_in_b, tm=n_seq)
    oa_s, st_s = _hgrn_sample(proj_s, lb_param, gn, state_hgrn[0], sb=8)
    ob_s = _moba_sample(proj_s.reshape(N_SECTIONS, n_seq, 1, D_MODEL),
                        k_s.reshape(n_seq, 1, D_MODEL), v_s.reshape(n_seq, 1, D_MODEL),
                        cache_k[0].reshape(n_pool * PAGE_SIZE * N_HEADS, HEAD_DIM),
                        cache_v[0].reshape(n_pool * PAGE_SIZE * N_HEADS, HEAD_DIM), page_table, group=8)
    h_s = _mix(oa_s, ob_s.reshape(n_seq, D_MODEL), proj_s, xs, wa, wb, wo, g1, b1, tm=n_seq)
    y_s = _mlp(h_s, wu, wd, g2, b2, tm=n_seq, tf=1024)

    kv_p = (DEPTH, batch, seq, N_HEADS, HEAD_DIM)
    kv_s = (DEPTH, n_seq, dec_seq, N_HEADS, HEAD_DIM)
    return (y_p.reshape(batch, seq, D_MODEL), y_s.reshape(n_seq, dec_seq, D_MODEL),
            k_p.reshape(kv_p), v_p.reshape(kv_p), st_p[None],
            k_s.reshape(kv_s), v_s.reshape(kv_s), st_s[None])
```

```python
import functools

import numpy as np
import jax
import jax.numpy as jnp
from jax import lax
from jax.experimental import pallas as pl
from jax.experimental.pallas import tpu as pltpu

F32 = jnp.float32
BF16 = jnp.bfloat16

D_MODEL = 1024
N_HEADS = 8
HEAD_DIM = 128
N_COLS = 9
COL_MK, COL_MV = 5, 6
N_SECTIONS = 7
SEC_Q, SEC_F, SEC_I, SEC_G, SEC_MQ, SEC_GA, SEC_GB = range(7)
DEPTH = 1
LN_EPS = 1e-5
RMS_EPS = 1e-6
DEEPNORM_ALPHA = (2.0 * DEPTH) ** 0.25
MOBA_BLOCK = 256
MOBA_TOPK = 3
PAGE_SIZE = 128
MASKED = -2e30
M_INIT = -1e30
LOG2E = 1.4426950408889634
SUBLANES = 8
STRIP = 128
HGRN_CHUNK = 128
VMEM_LIMIT = 48 * 1024 * 1024

_NT = (((1,), (1,)), ((), ()))


def _dot(a, b):
    return jnp.dot(a, b, preferred_element_type=F32)


def _dot_nt(a, b, precision=None):
    return lax.dot_general(a, b, _NT, precision=precision, preferred_element_type=F32)


def _params(n_grid):
    return pltpu.CompilerParams(dimension_semantics=("arbitrary",) * n_grid,
                                vmem_limit_bytes=VMEM_LIMIT)


def _inproj_kernel(x_ref, w_ref, o_ref, k_ref, v_ref):
    j = pl.program_id(1)
    y = _dot(x_ref[...].astype(BF16), w_ref[...])

    @pl.when(j == COL_MK)
    def _():
        k_ref[...] = y

    @pl.when(j == COL_MV)
    def _():
        v_ref[...] = y

    @pl.when((j != COL_MK) & (j != COL_MV))
    def _():
        o_ref[...] = y


def _inproj(x2d, w_bf16, tm):
    m = x2d.shape[0]

    def section(i, j):
        return (jnp.where(j < COL_MK, j, jnp.where(j > COL_MV, j - 2, COL_MK - 1)), i, 0)

    rows = pl.BlockSpec((tm, D_MODEL), lambda i, j: (i, 0))
    return pl.pallas_call(
        _inproj_kernel,
        grid=(m // tm, N_COLS),
        in_specs=[rows, pl.BlockSpec((D_MODEL, D_MODEL), lambda i, j: (0, j))],
        out_specs=[pl.BlockSpec((None, tm, D_MODEL), section), rows, rows],
        out_shape=[jax.ShapeDtypeStruct((N_SECTIONS, m, D_MODEL), F32),
                   jax.ShapeDtypeStruct((m, D_MODEL), F32),
                   jax.ShapeDtypeStruct((m, D_MODEL), F32)],
        compiler_params=_params(2),
        name="inproj",
    )(x2d, w_bf16)


def _lower_bound(p):
    m = p[0:1]
    for r in range(1, p.shape[0]):
        m = jnp.maximum(m, p[r:r + 1])
    e = [jnp.exp(p[r:r + 1] - m) for r in range(p.shape[0])]
    tot = e[0]
    for r in range(1, p.shape[0]):
        tot = tot + e[r]
    return e[0] / tot


def _gates_from_z(z, lb):
    e = jnp.exp(-jnp.abs(z))
    r = 1.0 / (1.0 + e)
    er = e * r
    pos = z >= 0
    oml = 1.0 - lb
    f = lb + oml * jnp.where(pos, r, er)
    k = oml * jnp.where(pos, er, r)
    return f, k


def _bc8(row):
    return jnp.broadcast_to(row, (SUBLANES, row.shape[-1]))


def _cumsum8(x):
    r = lax.broadcasted_iota(jnp.int32, x.shape, 0)
    for sh in (1, 2, 4):
        x = x + jnp.where(r >= sh, pltpu.roll(x, sh, 0), 0.0)
    return x


def _hgrn_chunk(qa, z, v, ga, lb, gnorm, st):
    n_tiles = HGRN_CHUNK // SUBLANES
    f, k = _gates_from_z(z, lb)
    logf = jnp.log2(f)
    q = qa * jax.nn.sigmoid(qa)

    def tiles(a):
        return [a[SUBLANES * j:SUBLANES * (j + 1)] for j in range(n_tiles)]

    qt, kt, vt = tiles(q), tiles(k), tiles(v)
    sc8 = [_cumsum8(t) for t in tiles(logf)]

    sc = list(sc8)
    levels = []
    nb = 1
    while nb < n_tiles:
        eq = [None] * n_tiles
        ek = [None] * n_tiles
        new = list(sc)
        for m in range(n_tiles // (2 * nb)):
            lo = range(2 * m * nb, (2 * m + 1) * nb)
            up = range((2 * m + 1) * nb, (2 * m + 2) * nb)
            tot = _bc8(sc[lo[-1]][SUBLANES - 1:SUBLANES])
            for j in lo:
                ek[j] = tot - sc[j]
            for j in up:
                eq[j] = sc[j]
                new[j] = sc[j] + tot
        levels.append((nb, eq, ek))
        sc = new
        nb *= 2
    a_t = sc
    a_last = a_t[n_tiles - 1][SUBLANES - 1:SUBLANES]

    zero = jnp.zeros((SUBLANES, HEAD_DIM), F32)
    row = lax.broadcasted_iota(jnp.int32, (HGRN_CHUNK, HGRN_CHUNK), 0)
    col = lax.broadcasted_iota(jnp.int32, (HGRN_CHUNK, HGRN_CHUNK), 1)
    scores = None
    for nb, eq, ek in levels:
        ql = jnp.concatenate([zero if eq[j] is None else qt[j] * jnp.exp2(eq[j]) for j in range(n_tiles)], axis=0)
        kl = jnp.concatenate([zero if ek[j] is None else kt[j] * jnp.exp2(ek[j]) for j in range(n_tiles)], axis=0)
        x = _dot_nt(ql.astype(BF16), kl.astype(BF16))
        group = 2 * nb * SUBLANES
        if group < HGRN_CHUNK:
            sh = group.bit_length() - 1
            x = jnp.where(lax.shift_right_logical(row, sh) == lax.shift_right_logical(col, sh), x, 0.0)
        scores = x if scores is None else scores + x

    r8 = lax.broadcasted_iota(jnp.int32, (SUBLANES, HEAD_DIM), 0)
    o_diag = []
    for j in range(n_tiles):
        a = sc8[j]
        acc = zero
        for s in range(SUBLANES):
            w = jnp.exp2(jnp.where(r8 >= s, a - _bc8(a[s:s + 1]), -jnp.inf))
            p = w * (qt[j] * _bc8(kt[j][s:s + 1]))
            acc = acc + jnp.sum(p, axis=-1, keepdims=True) * _bc8(vt[j][s:s + 1])
        o_diag.append(acc)
    o_diag = jnp.concatenate(o_diag, axis=0)

    qg = jnp.concatenate([qt[j] * jnp.exp2(a_t[j]) for j in range(n_tiles)], axis=0)
    kd = jnp.concatenate([kt[j] * jnp.exp2(_bc8(a_last) - a_t[j]) for j in range(n_tiles)], axis=0)
    o = _dot_nt(qg.astype(BF16), st.astype(BF16)) + _dot(scores.astype(BF16), v.astype(BF16)) + o_diag
    st_new = st * jnp.exp2(a_last) + _dot(v.T.astype(BF16), kd.astype(BF16))

    ms = jnp.mean(o * o, axis=-1, keepdims=True)
    o = o * lax.rsqrt(ms + RMS_EPS) * gnorm * (ga * jax.nn.sigmoid(ga))
    return o, st_new


def _hgrn_prompt_kernel(q_ref, f_ref, i_ref, g_ref, lbp_ref, gn_ref, o_ref, s_ref, st_scr, *, n_chunks):
    t = pl.program_id(1)

    @pl.when(t == 0)
    def _():
        st_scr[...] = jnp.zeros(st_scr.shape, F32)

    gnorm = gn_ref[...]

    def chunk_body(c, carry):
        rows = pl.ds(pl.multiple_of(c * HGRN_CHUNK, HGRN_CHUNK), HGRN_CHUNK)

        for h in range(N_HEADS):
            lanes = slice(h * HEAD_DIM, (h + 1) * HEAD_DIM)
            lb = _lower_bound(lbp_ref[:, lanes])
            o, st_new = _hgrn_chunk(q_ref[rows, lanes], f_ref[rows, lanes], i_ref[rows, lanes],
                                    g_ref[rows, lanes], lb, gnorm, st_scr[h])
            st_scr[h] = st_new
            o_ref[rows, lanes] = o.astype(o_ref.dtype)
        return carry

    lax.fori_loop(0, n_chunks, chunk_body, 0)

    @pl.when(t == pl.num_programs(1) - 1)
    def _():
        for h in range(N_HEADS):
            s_ref[h] = st_scr[h].T


def _hgrn_prompt(proj, lb_param, g_norm, batch, seq, tt):
    n_t = seq // tt

    def sec(s):
        return pl.BlockSpec((None, tt, D_MODEL), lambda b, t, s=s: (s, b * n_t + t, 0))

    return pl.pallas_call(
        functools.partial(_hgrn_prompt_kernel, n_chunks=tt // HGRN_CHUNK),
        grid=(batch, n_t),
        in_specs=[sec(SEC_Q), sec(SEC_F), sec(SEC_I), sec(SEC_G),
                  pl.BlockSpec(lb_param.shape, lambda b, t: (0, 0)),
                  pl.BlockSpec(g_norm.shape, lambda b, t: (0, 0))],
        out_specs=[pl.BlockSpec((tt, D_MODEL), lambda b, t: (b * n_t + t, 0)),
                   pl.BlockSpec((None, N_HEADS, HEAD_DIM, HEAD_DIM), lambda b, t: (b, 0, 0, 0))],
        out_shape=[jax.ShapeDtypeStruct((batch * seq, D_MODEL), BF16),
                   jax.ShapeDtypeStruct((batch, N_HEADS, HEAD_DIM, HEAD_DIM), F32)],
        scratch_shapes=[pltpu.VMEM((N_HEADS, HEAD_DIM, HEAD_DIM), F32)],
        compiler_params=_params(2),
        name="hgrn_prompt",
    )(proj, proj, proj, proj, lb_param, g_norm)


def _hgrn_sample_kernel(q_ref, f_ref, i_ref, g_ref, lbp_ref, gn_ref, s0_ref, o_ref, s_ref):
    sb = q_ref.shape[0]
    lb = _lower_bound(lbp_ref[...])
    f, k = _gates_from_z(f_ref[...], lb)
    qa = q_ref[...]
    q = qa * jax.nn.sigmoid(qa)
    ga = g_ref[...]
    gate = ga * jax.nn.sigmoid(ga)
    v = i_ref[...]
    gnorm = gn_ref[...]
    for b in range(sb):
        for h in range(N_HEADS):
            hl = slice(h * HEAD_DIM, (h + 1) * HEAD_DIM)
            f_col = f[b:b + 1, hl].T
            k_col = k[b:b + 1, hl].T
            q_col = q[b:b + 1, hl].T
            s_new = s0_ref[b, h] * f_col + k_col * v[b:b + 1, hl]
            s_ref[b, h] = s_new
            o = jnp.sum(s_new * q_col, axis=0, keepdims=True)
            ms = jnp.mean(o * o, axis=-1, keepdims=True)
            o_ref[b:b + 1, hl] = o * lax.rsqrt(ms + RMS_EPS) * gnorm * gate[b:b + 1, hl]


def _hgrn_sample(proj, lb_param, g_norm, state, sb):
    n = state.shape[0]

    def sec(s):
        return pl.BlockSpec((None, sb, D_MODEL), lambda i, s=s: (s, i, 0))

    st_spec = pl.BlockSpec((sb, N_HEADS, HEAD_DIM, HEAD_DIM), lambda i: (i, 0, 0, 0))
    return pl.pallas_call(
        _hgrn_sample_kernel,
        grid=(n // sb,),
        in_specs=[sec(SEC_Q), sec(SEC_F), sec(SEC_I), sec(SEC_G),
                  pl.BlockSpec(lb_param.shape, lambda i: (0, 0)),
                  pl.BlockSpec(g_norm.shape, lambda i: (0, 0)),
                  st_spec],
        out_specs=[pl.BlockSpec((sb, D_MODEL), lambda i: (i, 0)), st_spec],
        out_shape=[jax.ShapeDtypeStruct((n, D_MODEL), F32),
                   jax.ShapeDtypeStruct(state.shape, F32)],
        compiler_params=_params(1),
        name="hgrn_sample",
    )(proj, proj, proj, proj, lb_param, g_norm, state)


def _alibi_slopes():
    return np.asarray([2.0 ** (-8.0 * (h + 1) / N_HEADS) for h in range(N_HEADS)], np.float32)


def _topk_rank(g, idx, n):
    axis, count = n
    rank = jnp.zeros(g.shape, F32)
    for m in range(count):
        gm = lax.slice_in_dim(g, m, m + 1, axis=axis)
        rank = rank + ((gm > g) | ((gm == g) & (idx > m))).astype(F32)
    return rank


def _moba_prompt_kernel(slopes_ref, q_ref, k_ref, v_ref, o_ref,
                        kb, vt, qtb, kmean, bias, dist, *, n_blk):
    h = pl.program_id(1)
    blk = MOBA_BLOCK
    n_pairs = n_blk // 2
    slope2 = slopes_ref[h] * LOG2E
    scale2 = HEAD_DIM ** -0.5 * LOG2E

    def blk_rows(n):
        return pl.ds(pl.multiple_of(n * blk, blk), blk)

    def prep(n, carry):
        rows = blk_rows(n)
        kk = k_ref[rows, :]
        kb[rows, :] = kk.astype(BF16)
        kmean[pl.ds(n, 1), :] = jnp.sum(kk, axis=0, keepdims=True) * (1.0 / blk)
        vt[:, rows] = v_ref[rows, :].T.astype(BF16)
        return carry

    lax.fori_loop(0, n_blk, prep, 0)

    key = lax.broadcasted_iota(jnp.int32, (blk, blk), 0)
    qry = lax.broadcasted_iota(jnp.int32, (blk, blk), 1)
    d0 = slope2 * (qry - key).astype(F32)
    dist[0] = d0
    dist[1] = jnp.where(key <= qry, d0, -MASKED)

    def choose(qi, carry):
        rows = blk_rows(qi)
        q_t = q_ref[rows, :].T
        qtb[:, rows] = (q_t * scale2).astype(BF16)
        gate = jnp.dot(kmean[...], q_t, precision=lax.Precision.HIGHEST, preferred_element_type=F32)
        nio = lax.broadcasted_iota(jnp.int32, gate.shape, 0)
        past = nio < qi
        g = jnp.where(past, gate, -jnp.inf)
        attend = (past & (_topk_rank(g, nio, (0, n_blk)) < MOBA_TOPK)) | (nio == qi)
        bias[:, rows] = jnp.where(attend, -slope2 * ((qi - nio) * blk).astype(F32), MASKED)
        return carry

    lax.fori_loop(0, n_blk, choose, 0)

    for j in range(n_pairs):
        cols = slice(j * 2 * blk, (j + 1) * 2 * blk)
        m = jnp.full((1, 2 * blk), M_INIT, F32)
        l = jnp.zeros((1, 2 * blk), F32)
        acc = jnp.zeros((HEAD_DIM, 2 * blk), F32)
        for i in range(j + 1):
            keys = slice(i * 2 * blk, (i + 1) * 2 * blk)
            s = _dot(kb[keys, :], qtb[:, cols])
            own = 1 if i == j else 0
            ms, ls, alphas, ps = [], [], [], []
            for t in range(2 * blk // STRIP):
                lanes = slice(t * STRIP, (t + 1) * STRIP)
                in_blk = slice(t * STRIP % blk, t * STRIP % blk + STRIP)
                first = t * STRIP < blk
                xa = s[:blk, lanes] - dist[own if first else 0, :, in_blk]
                xb = s[blk:, lanes] - dist[0 if first else own, :, in_blk]
                ba = bias[2 * i:2 * i + 1, j * 2 * blk + t * STRIP:j * 2 * blk + (t + 1) * STRIP]
                bb = bias[2 * i + 1:2 * i + 2, j * 2 * blk + t * STRIP:j * 2 * blk + (t + 1) * STRIP]
                m_t = jnp.maximum(m[:, lanes], jnp.maximum(jnp.max(xa, axis=0, keepdims=True) + ba,
                                                           jnp.max(xb, axis=0, keepdims=True) + bb))
                alpha_t = jnp.exp2(m[:, lanes] - m_t)
                pa = jnp.exp2(xa - (m_t - ba))
                pb = jnp.exp2(xb - (m_t - bb))
                ls.append(alpha_t * l[:, lanes] + jnp.sum(pa, axis=0, keepdims=True)
                          + jnp.sum(pb, axis=0, keepdims=True))
                ms.append(m_t)
                alphas.append(alpha_t)
                ps.append(jnp.concatenate([pa.astype(BF16), pb.astype(BF16)], axis=0))
            m = jnp.concatenate(ms, axis=1)
            l = jnp.concatenate(ls, axis=1)
            acc = acc * jnp.concatenate(alphas, axis=1) + _dot(vt[:, keys], jnp.concatenate(ps, axis=1))
        o_ref[cols, :] = (acc * (1.0 / l)).T.astype(o_ref.dtype)


def _moba_prompt(proj, k_rows, v_rows, batch, seq):
    n_blk = seq // MOBA_BLOCK
    assert n_blk % 2 == 0, "key blocks are processed in pairs"
    head = pl.BlockSpec((seq, HEAD_DIM), lambda b, h: (b, h))
    return pl.pallas_call(
        functools.partial(_moba_prompt_kernel, n_blk=n_blk),
        grid=(batch, N_HEADS),
        in_specs=[pl.BlockSpec(memory_space=pltpu.SMEM),
                  pl.BlockSpec((None, seq, HEAD_DIM), lambda b, h: (SEC_MQ, b, h)),
                  head, head],
        out_specs=head,
        out_shape=jax.ShapeDtypeStruct((batch * seq, D_MODEL), BF16),
        scratch_shapes=[pltpu.VMEM((seq, HEAD_DIM), BF16),
                        pltpu.VMEM((HEAD_DIM, seq), BF16),
                        pltpu.VMEM((HEAD_DIM, seq), BF16),
                        pltpu.VMEM((n_blk, HEAD_DIM), F32),
                        pltpu.VMEM((n_blk, seq), F32),
                        pltpu.VMEM((2, MOBA_BLOCK, MOBA_BLOCK), F32)],
        compiler_params=_params(2),
        name="moba_prompt",
    )(jnp.asarray(_alibi_slopes()), proj, k_rows, v_rows)


def _page_heads_on_lanes(page_ref):
    return jnp.concatenate([page_ref[pl.ds(h, PAGE_SIZE, stride=N_HEADS), :] for h in range(N_HEADS)], axis=1)


def _moba_sample_kernel(pt_ref, slopes_ref, q_ref, kn_ref, vn_ref, *rest, n_pages, group):
    del pt_ref
    k_refs, v_refs = rest[:group], rest[group:2 * group]
    o_ref, sc, ksum, p_scr, acc, linv = rest[2 * group:]
    j = pl.program_id(1)
    k_steps = n_pages // group
    past_len = n_pages * PAGE_SIZE
    n_blk = past_len // MOBA_BLOCK
    pages_per_blk = MOBA_BLOCK // PAGE_SIZE
    blks_per_step = group // pages_per_blk
    scale = HEAD_DIM ** -0.5
    q = q_ref[...]
    head_of_lane = lax.shift_right_logical(
        lax.broadcasted_iota(jnp.int32, (N_HEADS, D_MODEL), 1), HEAD_DIM.bit_length() - 1)
    head_of_row = lax.broadcasted_iota(jnp.int32, (N_HEADS, D_MODEL), 0)
    own_lanes = head_of_lane == head_of_row
    qr = jnp.where(own_lanes, jnp.broadcast_to(q, (N_HEADS, D_MODEL)), 0.0)

    @pl.when(j < k_steps)
    def _():
        qrb = qr.astype(BF16)
        sums = []
        for i in range(group):
            kp = _page_heads_on_lanes(k_refs[i])
            page = j * group + i
            sc[:, pl.ds(pl.multiple_of(page * PAGE_SIZE, PAGE_SIZE), PAGE_SIZE)] = _dot_nt(qrb, kp.astype(BF16))
            sums.append(jnp.sum(kp, axis=0, keepdims=True))
        for b in range(blks_per_step):
            tot = sums[b * pages_per_blk]
            for r in range(1, pages_per_blk):
                tot = tot + sums[b * pages_per_blk + r]
            ksum[pl.ds(j * blks_per_step + b, 1), :] = tot

    @pl.when(j == k_steps - 1)
    def _():
        kmean = ksum[...] * (1.0 / MOBA_BLOCK)
        gate = _dot_nt(qr, kmean, precision=lax.Precision.HIGHEST)
        nio = lax.broadcasted_iota(jnp.int32, gate.shape, 1)
        sel = _topk_rank(gate, nio, (1, n_blk)) < MOBA_TOPK
        kpos = lax.broadcasted_iota(jnp.int32, (N_HEADS, past_len), 1)
        blk_of_key = lax.shift_right_logical(kpos, MOBA_BLOCK.bit_length() - 1)
        chosen = jnp.zeros((N_HEADS, past_len), jnp.bool_)
        for n in range(n_blk):
            chosen = chosen | ((blk_of_key == n) & sel[:, n:n + 1])
        slope = slopes_ref[...]
        s_all = sc[...] * scale - slope * (past_len - kpos).astype(F32)
        s_all = jnp.where(chosen, s_all, MASKED)
        s_self = jnp.sum(qr * kn_ref[...], axis=-1, keepdims=True) * scale
        m = jnp.maximum(jnp.max(s_all, axis=-1, keepdims=True), s_self)
        p = jnp.exp(s_all - m)
        p_self = jnp.exp(s_self - m)
        p_scr[...] = p
        linv[...] = jnp.broadcast_to(1.0 / (jnp.sum(p, axis=-1, keepdims=True) + p_self), linv.shape)
        acc[...] = p_self * jnp.broadcast_to(vn_ref[...], (N_HEADS, D_MODEL))

    @pl.when(j >= k_steps)
    def _():
        part = None
        for i in range(group):
            page = (j - k_steps) * group + i
            pj = p_scr[:, pl.ds(pl.multiple_of(page * PAGE_SIZE, PAGE_SIZE), PAGE_SIZE)]
            d = _dot(pj.astype(BF16), _page_heads_on_lanes(v_refs[i]).astype(BF16))
            part = d if part is None else part + d
        acc[...] += part

    @pl.when(j == 2 * k_steps - 1)
    def _():
        a = acc[...] * linv[:, 0:1]
        for h in range(N_HEADS):
            hl = slice(h * HEAD_DIM, (h + 1) * HEAD_DIM)
            o_ref[:, hl] = a[h:h + 1, hl]


def _moba_sample(proj3, k_new, v_new, cache_k, cache_v, page_table, group):
    n_seq, n_pages = page_table.shape
    past_len = n_pages * PAGE_SIZE
    assert past_len % MOBA_BLOCK == 0, "the decode token must start a MoBA block"
    assert n_pages % group == 0 and group % (MOBA_BLOCK // PAGE_SIZE) == 0
    n_blk = past_len // MOBA_BLOCK
    k_steps = n_pages // group
    page_rows = PAGE_SIZE * N_HEADS
    new_row = pl.BlockSpec((None, 1, D_MODEL), lambda i, j, pt: (i, 0, 0))

    def k_page(g):
        return pl.BlockSpec((page_rows, HEAD_DIM), lambda i, j, pt: (
            pt[i * n_pages + jnp.minimum(j, k_steps - 1) * group + g], 0))

    def v_page(g):
        def index(i, j, pt):
            in_k = j < k_steps
            seq_i = jnp.maximum(i - jnp.where(in_k, 1, 0), 0)
            grp = jnp.where(in_k, k_steps - 1, j - k_steps)
            return pt[seq_i * n_pages + grp * group + g], 0
        return pl.BlockSpec((page_rows, HEAD_DIM), index)

    grid_spec = pltpu.PrefetchScalarGridSpec(
        num_scalar_prefetch=1,
        grid=(n_seq, 2 * k_steps),
        in_specs=[pl.BlockSpec((N_HEADS, 1), lambda i, j, pt: (0, 0)),
                  pl.BlockSpec((None, None, 1, D_MODEL), lambda i, j, pt: (SEC_MQ, i, 0, 0)), new_row, new_row]
                 + [k_page(g) for g in range(group)] + [v_page(g) for g in range(group)],
        out_specs=new_row,
        scratch_shapes=[pltpu.VMEM((N_HEADS, past_len), F32),
                        pltpu.VMEM((n_blk, D_MODEL), F32),
                        pltpu.VMEM((N_HEADS, past_len), F32),
                        pltpu.VMEM((N_HEADS, D_MODEL), F32),
                        pltpu.VMEM((N_HEADS, HEAD_DIM), F32)])
    return pl.pallas_call(
        functools.partial(_moba_sample_kernel, n_pages=n_pages, group=group),
        grid_spec=grid_spec,
        out_shape=jax.ShapeDtypeStruct((n_seq, 1, D_MODEL), F32),
        compiler_params=_params(2),
        name="moba_sample",
    )(page_table.reshape(-1), jnp.asarray(_alibi_slopes()).reshape(N_HEADS, 1),
      proj3, k_new, v_new, *([cache_k] * group), *([cache_v] * group))


def _layer_norm(x, g, b):
    mu = jnp.mean(x, axis=-1, keepdims=True)
    xc = x - mu
    var = jnp.mean(xc * xc, axis=-1, keepdims=True)
    return xc * lax.rsqrt(var + LN_EPS) * g + b


def _mix_kernel(oa_ref, ob_ref, ga_ref, gb_ref, x_ref, wa_ref, wb_ref, wo_ref, g_ref, b_ref, h_ref):
    ya = _dot(oa_ref[...].astype(BF16), wa_ref[...])
    yb = _dot(ob_ref[...].astype(BF16), wb_ref[...])
    mixed = jax.nn.sigmoid(ga_ref[...]) * ya + jax.nn.sigmoid(gb_ref[...]) * yb
    mix = _dot(mixed.astype(BF16), wo_ref[...])
    h_ref[...] = _layer_norm(DEEPNORM_ALPHA * x_ref[...] + mix, g_ref[...], b_ref[...])


def _mix(o_a, o_b, proj, x2d, wa, wb, wo, ln_g, ln_b, tm):
    m = x2d.shape[0]
    rows = pl.BlockSpec((tm, D_MODEL), lambda i: (i, 0))
    sec = lambda s: pl.BlockSpec((None, tm, D_MODEL), lambda i, s=s: (s, i, 0))
    full = lambda a: pl.BlockSpec(a.shape, lambda i: (0, 0))
    return pl.pallas_call(
        _mix_kernel,
        grid=(m // tm,),
        in_specs=[rows, rows, sec(SEC_GA), sec(SEC_GB), rows, full(wa), full(wb), full(wo), full(ln_g), full(ln_b)],
        out_specs=rows,
        out_shape=jax.ShapeDtypeStruct((m, D_MODEL), F32),
        compiler_params=_params(1),
        name="mix",
    )(o_a, o_b, proj, proj, x2d, wa, wb, wo, ln_g, ln_b)


def _mlp_kernel(h_ref, wu_ref, wd_ref, g_ref, b_ref, y_ref, acc):
    j = pl.program_id(1)
    h = h_ref[...]
    u = jnp.maximum(_dot(h.astype(BF16), wu_ref[...]), 0.0)
    part = _dot((u * u).astype(BF16), wd_ref[...])

    @pl.when(j == 0)
    def _():
        acc[...] = part

    @pl.when(j > 0)
    def _():
        acc[...] += part

    @pl.when(j == pl.num_programs(1) - 1)
    def _():
        y_ref[...] = _layer_norm(DEEPNORM_ALPHA * h + acc[...], g_ref[...], b_ref[...])


def _mlp(h, wu, wd, ln_g, ln_b, tm, tf):
    m = h.shape[0]
    d_ff = wu.shape[1]
    rows = pl.BlockSpec((tm, D_MODEL), lambda i, j: (i, 0))
    full = lambda a: pl.BlockSpec(a.shape, lambda i, j: (0, 0))
    return pl.pallas_call(
        _mlp_kernel,
        grid=(m // tm, d_ff // tf),
        in_specs=[rows,
                  pl.BlockSpec((D_MODEL, tf), lambda i, j: (0, j)),
                  pl.BlockSpec((tf, D_MODEL), lambda i, j: (j, 0)),
                  full(ln_g), full(ln_b)],
        out_specs=rows,
        out_shape=jax.ShapeDtypeStruct((m, D_MODEL), F32),
        scratch_shapes=[pltpu.VMEM((tm, D_MODEL), F32)],
        compiler_params=_params(2),
        name="mlp",
    )(h, wu, wd, ln_g, ln_b)


def kernel(x_prompt, x_sample, cache_k, cache_v, state_hgrn, page_table, w_in, lb_param, g_norm_a,
           w_proj_a, w_proj_b, w_out, ln1_g, ln1_b, w_up, w_down, ln2_g, ln2_b):
    assert w_in.shape[0] == DEPTH
    batch, seq, _ = x_prompt.shape
    n_seq, dec_seq, _ = x_sample.shape
    assert dec_seq == 1
    n_pool = cache_k.shape[1]

    w_in_b = w_in[0].astype(BF16)
    wa, wb, wo = w_proj_a[0].astype(BF16), w_proj_b[0].astype(BF16), w_out[0].astype(BF16)
    wu, wd = w_up[0].astype(BF16), w_down[0].astype(BF16)
    gn = g_norm_a[0].reshape(1, HEAD_DIM)
    g1, b1 = ln1_g[0].reshape(1, D_MODEL), ln1_b[0].reshape(1, D_MODEL)
    g2, b2 = ln2_g[0].reshape(1, D_MODEL), ln2_b[0].reshape(1, D_MODEL)

    xp = x_prompt.reshape(batch * seq, D_MODEL)
    proj_p, k_p, v_p = _inproj(xp, w_in_b, tm=1024)
    oa_p, st_p = _hgrn_prompt(proj_p, lb_param, gn, batch, seq, tt=512)
    ob_p = _moba_prompt(proj_p, k_p, v_p, batch, seq)
    h_p = _mix(oa_p, ob_p, proj_p, xp, wa, wb, wo, g1, b1, tm=512)
    y_p = _mlp(h_p, wu, wd, g2, b2, tm=1024, tf=1024)

    xs = x_sample.reshape(n_seq, D_MODEL)
    proj_s, k_s, v_s = _inproj(xs, w_in_b, tm=n_seq)
    oa_s, st_s = _hgrn_sample(proj_s, lb_param, gn, state_hgrn[0], sb=8)
    ob_s = _moba_sample(proj_s.reshape(N_SECTIONS, n_seq, 1, D_MODEL),
                        k_s.reshape(n_seq, 1, D_MODEL), v_s.reshape(n_seq, 1, D_MODEL),
                        cache_k[0].reshape(n_pool * PAGE_SIZE * N_HEADS, HEAD_DIM),
                        cache_v[0].reshape(n_pool * PAGE_SIZE * N_HEADS, HEAD_DIM), page_table, group=16)
    h_s = _mix(oa_s, ob_s.reshape(n_seq, D_MODEL), proj_s, xs, wa, wb, wo, g1, b1, tm=n_seq)
    y_s = _mlp(h_s, wu, wd, g2, b2, tm=n_seq, tf=1024)

    kv_p = (DEPTH, batch, seq, N_HEADS, HEAD_DIM)
    kv_s = (DEPTH, n_seq, dec_seq, N_HEADS, HEAD_DIM)
    return (y_p.reshape(batch, seq, D_MODEL), y_s.reshape(n_seq, dec_seq, D_MODEL),
            k_p.reshape(kv_p), v_p.reshape(kv_p), st_p[None],
            k_s.reshape(kv_s), v_s.reshape(kv_s), st_s[None])
```

```python
import functools

import numpy as np
import jax
import jax.numpy as jnp
from jax import lax
from jax.experimental import pallas as pl
from jax.experimental.pallas import tpu as pltpu

F32 = jnp.float32
BF16 = jnp.bfloat16

D_MODEL = 1024
N_HEADS = 8
HEAD_DIM = 128
N_COLS = 9
COL_MK, COL_MV = 5, 6
N_WIDE = 2
WIDE_Q, WIDE_F = range(N_WIDE)
N_NARROW = 5
NARROW_I, NARROW_G, NARROW_MQ, NARROW_GA, NARROW_GB = range(N_NARROW)
DEPTH = 1
LN_EPS = 1e-5
RMS_EPS = 1e-6
DEEPNORM_ALPHA = (2.0 * DEPTH) ** 0.25
MOBA_BLOCK = 256
MOBA_TOPK = 3
PAGE_SIZE = 128
MASKED = -2e30
M_INIT = -1e30
LOG2E = 1.4426950408889634
SUBLANES = 8
STRIP = 128
HGRN_CHUNK = 128
VMEM_LIMIT = 48 * 1024 * 1024

_NT = (((1,), (1,)), ((), ()))


def _dot(a, b):
    return jnp.dot(a, b, preferred_element_type=F32)


def _dot_nt(a, b, precision=None):
    return lax.dot_general(a, b, _NT, precision=precision, preferred_element_type=F32)


def _params(n_grid):
    return pltpu.CompilerParams(dimension_semantics=("arbitrary",) * n_grid,
                                vmem_limit_bytes=VMEM_LIMIT)


def _inproj_kernel(x_ref, w_ref, wide_ref, narrow_ref, k_ref, v_ref):
    j = pl.program_id(1)
    y = _dot(x_ref[...].astype(BF16), w_ref[...])

    @pl.when(j < N_WIDE)
    def _():
        wide_ref[...] = y

    @pl.when(j == COL_MK)
    def _():
        k_ref[...] = y

    @pl.when(j == COL_MV)
    def _():
        v_ref[...] = y

    @pl.when((j >= N_WIDE) & (j != COL_MK) & (j != COL_MV))
    def _():
        narrow_ref[...] = y.astype(narrow_ref.dtype)


def _inproj(x2d, w_bf16, tm):
    m = x2d.shape[0]

    def wide(i, j):
        return jnp.minimum(j, N_WIDE - 1), i, 0

    def narrow(i, j):
        sec = jnp.where(j < COL_MK, j - N_WIDE, jnp.where(j > COL_MV, j - N_WIDE - 2, COL_MK - 1 - N_WIDE))
        return jnp.maximum(sec, 0), i, 0

    rows = pl.BlockSpec((tm, D_MODEL), lambda i, j: (i, 0))
    return pl.pallas_call(
        _inproj_kernel,
        grid=(m // tm, N_COLS),
        in_specs=[rows, pl.BlockSpec((D_MODEL, D_MODEL), lambda i, j: (0, j))],
        out_specs=[pl.BlockSpec((None, tm, D_MODEL), wide), pl.BlockSpec((None, tm, D_MODEL), narrow), rows, rows],
        out_shape=[jax.ShapeDtypeStruct((N_WIDE, m, D_MODEL), F32),
                   jax.ShapeDtypeStruct((N_NARROW, m, D_MODEL), BF16),
                   jax.ShapeDtypeStruct((m, D_MODEL), F32),
                   jax.ShapeDtypeStruct((m, D_MODEL), F32)],
        compiler_params=_params(2),
        name="inproj",
    )(x2d, w_bf16)


def _lower_bound(p):
    m = p[0:1]
    for r in range(1, p.shape[0]):
        m = jnp.maximum(m, p[r:r + 1])
    e = [jnp.exp(p[r:r + 1] - m) for r in range(p.shape[0])]
    tot = e[0]
    for r in range(1, p.shape[0]):
        tot = tot + e[r]
    return e[0] / tot


def _gates_from_z(z, lb):
    e = jnp.exp(-jnp.abs(z))
    r = 1.0 / (1.0 + e)
    er = e * r
    pos = z >= 0
    oml = 1.0 - lb
    f = lb + oml * jnp.where(pos, r, er)
    k = oml * jnp.where(pos, er, r)
    return f, k


def _bc8(row):
    return jnp.broadcast_to(row, (SUBLANES, row.shape[-1]))


def _cumsum8(x):
    r = lax.broadcasted_iota(jnp.int32, x.shape, 0)
    for sh in (1, 2, 4):
        x = x + jnp.where(r >= sh, pltpu.roll(x, sh, 0), 0.0)
    return x


def _hgrn_chunk(qa, z, v, ga, lb, gnorm, st):
    n_tiles = HGRN_CHUNK // SUBLANES
    f, k = _gates_from_z(z, lb)
    logf = jnp.log2(f)
    q = qa * jax.nn.sigmoid(qa)

    def tiles(a):
        return [a[SUBLANES * j:SUBLANES * (j + 1)] for j in range(n_tiles)]

    qt, kt, vt = tiles(q), tiles(k), tiles(v)
    sc8 = [_cumsum8(t) for t in tiles(logf)]

    sc = list(sc8)
    levels = []
    nb = 1
    while nb < n_tiles:
        eq = [None] * n_tiles
        ek = [None] * n_tiles
        new = list(sc)
        for m in range(n_tiles // (2 * nb)):
            lo = range(2 * m * nb, (2 * m + 1) * nb)
            up = range((2 * m + 1) * nb, (2 * m + 2) * nb)
            tot = _bc8(sc[lo[-1]][SUBLANES - 1:SUBLANES])
            for j in lo:
                ek[j] = tot - sc[j]
            for j in up:
                eq[j] = sc[j]
                new[j] = sc[j] + tot
        levels.append((nb, eq, ek))
        sc = new
        nb *= 2
    a_t = sc
    a_last = a_t[n_tiles - 1][SUBLANES - 1:SUBLANES]

    zero = jnp.zeros((SUBLANES, HEAD_DIM), F32)
    row = lax.broadcasted_iota(jnp.int32, (HGRN_CHUNK, HGRN_CHUNK), 0)
    col = lax.broadcasted_iota(jnp.int32, (HGRN_CHUNK, HGRN_CHUNK), 1)
    scores = None
    for nb, eq, ek in levels:
        ql = jnp.concatenate([zero if eq[j] is None else qt[j] * jnp.exp2(eq[j]) for j in range(n_tiles)], axis=0)
        kl = jnp.concatenate([zero if ek[j] is None else kt[j] * jnp.exp2(ek[j]) for j in range(n_tiles)], axis=0)
        x = _dot_nt(ql.astype(BF16), kl.astype(BF16))
        group = 2 * nb * SUBLANES
        if group < HGRN_CHUNK:
            sh = group.bit_length() - 1
            x = jnp.where(lax.shift_right_logical(row, sh) == lax.shift_right_logical(col, sh), x, 0.0)
        scores = x if scores is None else scores + x

    r8 = lax.broadcasted_iota(jnp.int32, (SUBLANES, HEAD_DIM), 0)
    o_diag = []
    for j in range(n_tiles):
        a = sc8[j]
        acc = zero
        for s in range(SUBLANES):
            w = jnp.exp2(jnp.where(r8 >= s, a - _bc8(a[s:s + 1]), -jnp.inf))
            p = w * (qt[j] * _bc8(kt[j][s:s + 1]))
            acc = acc + jnp.sum(p, axis=-1, keepdims=True) * _bc8(vt[j][s:s + 1])
        o_diag.append(acc)
    o_diag = jnp.concatenate(o_diag, axis=0)

    qg = jnp.concatenate([qt[j] * jnp.exp2(a_t[j]) for j in range(n_tiles)], axis=0)
    kd = jnp.concatenate([kt[j] * jnp.exp2(_bc8(a_last) - a_t[j]) for j in range(n_tiles)], axis=0)
    o = _dot_nt(qg.astype(BF16), st.astype(BF16)) + _dot(scores.astype(BF16), v.astype(BF16)) + o_diag
    st_new = st * jnp.exp2(a_last) + _dot(v.T.astype(BF16), kd.astype(BF16))

    ms = jnp.mean(o * o, axis=-1, keepdims=True)
    o = o * lax.rsqrt(ms + RMS_EPS) * gnorm * (ga * jax.nn.sigmoid(ga))
    return o, st_new


def _hgrn_prompt_kernel(q_ref, f_ref, i_ref, g_ref, lbp_ref, gn_ref, o_ref, s_ref, st_scr, *, n_chunks):
    t = pl.program_id(1)

    @pl.when(t == 0)
    def _():
        st_scr[...] = jnp.zeros(st_scr.shape, F32)

    gnorm = gn_ref[...]

    def chunk_body(c, carry):
        rows = pl.ds(pl.multiple_of(c * HGRN_CHUNK, HGRN_CHUNK), HGRN_CHUNK)

        for h in range(N_HEADS):
            lanes = slice(h * HEAD_DIM, (h + 1) * HEAD_DIM)
            lb = _lower_bound(lbp_ref[:, lanes])
            o, st_new = _hgrn_chunk(q_ref[rows, lanes], f_ref[rows, lanes], i_ref[rows, lanes].astype(F32),
                                    g_ref[rows, lanes].astype(F32), lb, gnorm, st_scr[h])
            st_scr[h] = st_new
            o_ref[rows, lanes] = o.astype(o_ref.dtype)
        return carry

    lax.fori_loop(0, n_chunks, chunk_body, 0)

    @pl.when(t == pl.num_programs(1) - 1)
    def _():
        for h in range(N_HEADS):
            s_ref[h] = st_scr[h].T


def _hgrn_prompt(wide, narrow, lb_param, g_norm, batch, seq, tt):
    n_t = seq // tt

    def sec(s):
        return pl.BlockSpec((None, tt, D_MODEL), lambda b, t, s=s: (s, b * n_t + t, 0))

    return pl.pallas_call(
        functools.partial(_hgrn_prompt_kernel, n_chunks=tt // HGRN_CHUNK),
        grid=(batch, n_t),
        in_specs=[sec(WIDE_Q), sec(WIDE_F), sec(NARROW_I), sec(NARROW_G),
                  pl.BlockSpec(lb_param.shape, lambda b, t: (0, 0)),
                  pl.BlockSpec(g_norm.shape, lambda b, t: (0, 0))],
        out_specs=[pl.BlockSpec((tt, D_MODEL), lambda b, t: (b * n_t + t, 0)),
                   pl.BlockSpec((None, N_HEADS, HEAD_DIM, HEAD_DIM), lambda b, t: (b, 0, 0, 0))],
        out_shape=[jax.ShapeDtypeStruct((batch * seq, D_MODEL), BF16),
                   jax.ShapeDtypeStruct((batch, N_HEADS, HEAD_DIM, HEAD_DIM), F32)],
        scratch_shapes=[pltpu.VMEM((N_HEADS, HEAD_DIM, HEAD_DIM), F32)],
        compiler_params=_params(2),
        name="hgrn_prompt",
    )(wide, wide, narrow, narrow, lb_param, g_norm)


def _hgrn_sample_kernel(q_ref, f_ref, i_ref, g_ref, lbp_ref, gn_ref, s0_ref, o_ref, s_ref):
    sb = q_ref.shape[0]
    lb = _lower_bound(lbp_ref[...])
    f, k = _gates_from_z(f_ref[...], lb)
    qa = q_ref[...]
    q = qa * jax.nn.sigmoid(qa)
    ga = g_ref[...]
    gate = ga * jax.nn.sigmoid(ga)
    v = i_ref[...]
    gnorm = gn_ref[...]
    for b in range(sb):
        for h in range(N_HEADS):
            hl = slice(h * HEAD_DIM, (h + 1) * HEAD_DIM)
            f_col = f[b:b + 1, hl].T
            k_col = k[b:b + 1, hl].T
            q_col = q[b:b + 1, hl].T
            s_new = s0_ref[b, h] * f_col + k_col * v[b:b + 1, hl]
            s_ref[b, h] = s_new
            o = jnp.sum(s_new * q_col, axis=0, keepdims=True)
            ms = jnp.mean(o * o, axis=-1, keepdims=True)
            o_ref[b:b + 1, hl] = o * lax.rsqrt(ms + RMS_EPS) * gnorm * gate[b:b + 1, hl]


def _hgrn_sample(wide, narrow, lb_param, g_norm, state, sb):
    n = state.shape[0]

    def sec(s):
        return pl.BlockSpec((None, sb, D_MODEL), lambda i, s=s: (s, i, 0))

    st_spec = pl.BlockSpec((sb, N_HEADS, HEAD_DIM, HEAD_DIM), lambda i: (i, 0, 0, 0))
    return pl.pallas_call(
        _hgrn_sample_kernel,
        grid=(n // sb,),
        in_specs=[sec(WIDE_Q), sec(WIDE_F), sec(NARROW_I), sec(NARROW_G),
                  pl.BlockSpec(lb_param.shape, lambda i: (0, 0)),
                  pl.BlockSpec(g_norm.shape, lambda i: (0, 0)),
                  st_spec],
        out_specs=[pl.BlockSpec((sb, D_MODEL), lambda i: (i, 0)), st_spec],
        out_shape=[jax.ShapeDtypeStruct((n, D_MODEL), F32),
                   jax.ShapeDtypeStruct(state.shape, F32)],
        compiler_params=_params(1),
        name="hgrn_sample",
    )(wide, wide, narrow, narrow, lb_param, g_norm, state)


def _alibi_slopes():
    return np.asarray([2.0 ** (-8.0 * (h + 1) / N_HEADS) for h in range(N_HEADS)], np.float32)


def _topk_rank(g, idx, n):
    axis, count = n
    rank = jnp.zeros(g.shape, F32)
    for m in range(count):
        gm = lax.slice_in_dim(g, m, m + 1, axis=axis)
        rank = rank + ((gm > g) | ((gm == g) & (idx > m))).astype(F32)
    return rank


def _moba_prompt_kernel(slopes_ref, q_ref, k_ref, v_ref, o_ref,
                        kb, vt, qtb, kmean, bias, dist, *, n_blk):
    h = pl.program_id(1)
    blk = MOBA_BLOCK
    n_pairs = n_blk // 2
    slope2 = slopes_ref[h] * LOG2E
    scale2 = HEAD_DIM ** -0.5 * LOG2E

    def blk_rows(n):
        return slice(n * blk, (n + 1) * blk)

    def prep(n, carry):
        rows = blk_rows(n)
        kk = k_ref[rows, :]
        kb[rows, :] = kk.astype(BF16)
        kmean[pl.ds(n, 1), :] = jnp.sum(kk, axis=0, keepdims=True) * (1.0 / blk)
        vt[:, rows] = v_ref[rows, :].T.astype(BF16)
        return carry

    for n in range(n_blk):
        prep(n, 0)

    key = lax.broadcasted_iota(jnp.int32, (blk, blk), 0)
    qry = lax.broadcasted_iota(jnp.int32, (blk, blk), 1)
    d0 = slope2 * (qry - key).astype(F32)
    dist[0] = d0
    dist[1] = jnp.where(key <= qry, d0, -MASKED)

    def choose(qi, carry):
        rows = blk_rows(qi)
        q_t = q_ref[rows, :].astype(F32).T
        qtb[:, rows] = (q_t * scale2).astype(BF16)
        gate = jnp.dot(kmean[...], q_t, precision=lax.Precision.HIGHEST, preferred_element_type=F32)
        nio = lax.broadcasted_iota(jnp.int32, gate.shape, 0)
        past = nio < qi
        g = jnp.where(past, gate, -jnp.inf)
        attend = (past & (_topk_rank(g, nio, (0, n_blk)) < MOBA_TOPK)) | (nio == qi)
        bias[:, rows] = jnp.where(attend, -slope2 * ((qi - nio) * blk).astype(F32), MASKED)
        return carry

    for qi in range(n_blk):
        choose(qi, 0)

    for j in range(n_pairs):
        cols = slice(j * 2 * blk, (j + 1) * 2 * blk)
        m = jnp.full((1, 2 * blk), M_INIT, F32)
        l = jnp.zeros((1, 2 * blk), F32)
        acc = jnp.zeros((HEAD_DIM, 2 * blk), F32)
        for i in range(j + 1):
            keys = slice(i * 2 * blk, (i + 1) * 2 * blk)
            s = _dot(kb[keys, :], qtb[:, cols])
            own = 1 if i == j else 0
            ms, ls, alphas, ps = [], [], [], []
            for t in range(2 * blk // STRIP):
                lanes = slice(t * STRIP, (t + 1) * STRIP)
                in_blk = slice(t * STRIP % blk, t * STRIP % blk + STRIP)
                first = t * STRIP < blk
                xa = s[:blk, lanes] - dist[own if first else 0, :, in_blk]
                xb = s[blk:, lanes] - dist[0 if first else own, :, in_blk]
                ba = bias[2 * i:2 * i + 1, j * 2 * blk + t * STRIP:j * 2 * blk + (t + 1) * STRIP]
                bb = bias[2 * i + 1:2 * i + 2, j * 2 * blk + t * STRIP:j * 2 * blk + (t + 1) * STRIP]
                m_t = jnp.maximum(m[:, lanes], jnp.maximum(jnp.max(xa, axis=0, keepdims=True) + ba,
                                                           jnp.max(xb, axis=0, keepdims=True) + bb))
                alpha_t = jnp.exp2(m[:, lanes] - m_t)
                pa = jnp.exp2(xa - (m_t - ba))
                pb = jnp.exp2(xb - (m_t - bb))
                ls.append(alpha_t * l[:, lanes] + jnp.sum(pa, axis=0, keepdims=True)
                          + jnp.sum(pb, axis=0, keepdims=True))
                ms.append(m_t)
                alphas.append(alpha_t)
                ps.append(jnp.concatenate([pa.astype(BF16), pb.astype(BF16)], axis=0))
            m = jnp.concatenate(ms, axis=1)
            l = jnp.concatenate(ls, axis=1)
            acc = acc * jnp.concatenate(alphas, axis=1) + _dot(vt[:, keys], jnp.concatenate(ps, axis=1))
        o_ref[cols, :] = (acc * (1.0 / l)).T.astype(o_ref.dtype)


def _moba_prompt(narrow, k_rows, v_rows, batch, seq):
    n_blk = seq // MOBA_BLOCK
    assert n_blk % 2 == 0, "key blocks are processed in pairs"
    head = pl.BlockSpec((seq, HEAD_DIM), lambda b, h: (b, h))
    return pl.pallas_call(
        functools.partial(_moba_prompt_kernel, n_blk=n_blk),
        grid=(batch, N_HEADS),
        in_specs=[pl.BlockSpec(memory_space=pltpu.SMEM),
                  pl.BlockSpec((None, seq, HEAD_DIM), lambda b, h: (NARROW_MQ, b, h)),
                  head, head],
        out_specs=head,
        out_shape=jax.ShapeDtypeStruct((batch * seq, D_MODEL), BF16),
        scratch_shapes=[pltpu.VMEM((seq, HEAD_DIM), BF16),
                        pltpu.VMEM((HEAD_DIM, seq), BF16),
                        pltpu.VMEM((HEAD_DIM, seq), BF16),
                        pltpu.VMEM((n_blk, HEAD_DIM), F32),
                        pltpu.VMEM((n_blk, seq), F32),
                        pltpu.VMEM((2, MOBA_BLOCK, MOBA_BLOCK), F32)],
        compiler_params=_params(2),
        name="moba_prompt",
    )(jnp.asarray(_alibi_slopes()), narrow, k_rows, v_rows)


def _page_heads_on_lanes(page_ref):
    return jnp.concatenate([page_ref[pl.ds(h, PAGE_SIZE, stride=N_HEADS), :] for h in range(N_HEADS)], axis=1)


def _moba_sample_kernel(pt_ref, slopes_ref, q_ref, kn_ref, vn_ref, *rest, n_pages, group):
    del pt_ref
    k_refs, v_refs = rest[:group], rest[group:2 * group]
    o_ref, sc, ksum, p_scr, acc, linv = rest[2 * group:]
    j = pl.program_id(1)
    k_steps = n_pages // group
    past_len = n_pages * PAGE_SIZE
    n_blk = past_len // MOBA_BLOCK
    pages_per_blk = MOBA_BLOCK // PAGE_SIZE
    blks_per_step = group // pages_per_blk
    scale = HEAD_DIM ** -0.5
    q = q_ref[...].astype(F32)
    head_of_lane = lax.shift_right_logical(
        lax.broadcasted_iota(jnp.int32, (N_HEADS, D_MODEL), 1), HEAD_DIM.bit_length() - 1)
    head_of_row = lax.broadcasted_iota(jnp.int32, (N_HEADS, D_MODEL), 0)
    own_lanes = head_of_lane == head_of_row
    qr = jnp.where(own_lanes, jnp.broadcast_to(q, (N_HEADS, D_MODEL)), 0.0)

    @pl.when(j < k_steps)
    def _():
        qrb = qr.astype(BF16)
        sums = []
        for i in range(group):
            kp = _page_heads_on_lanes(k_refs[i])
            page = j * group + i
            sc[:, pl.ds(pl.multiple_of(page * PAGE_SIZE, PAGE_SIZE), PAGE_SIZE)] = _dot_nt(qrb, kp.astype(BF16))
            sums.append(jnp.sum(kp, axis=0, keepdims=True))
        for b in range(blks_per_step):
            tot = sums[b * pages_per_blk]
            for r in range(1, pages_per_blk):
                tot = tot + sums[b * pages_per_blk + r]
            ksum[pl.ds(j * blks_per_step + b, 1), :] = tot

    @pl.when(j == k_steps - 1)
    def _():
        kmean = ksum[...] * (1.0 / MOBA_BLOCK)
        gate = _dot_nt(qr, kmean, precision=lax.Precision.HIGHEST)
        nio = lax.broadcasted_iota(jnp.int32, gate.shape, 1)
        sel = _topk_rank(gate, nio, (1, n_blk)) < MOBA_TOPK
        kpos = lax.broadcasted_iota(jnp.int32, (N_HEADS, past_len), 1)
        blk_of_key = lax.shift_right_logical(kpos, MOBA_BLOCK.bit_length() - 1)
        chosen = jnp.zeros((N_HEADS, past_len), jnp.bool_)
        for n in range(n_blk):
            chosen = chosen | ((blk_of_key == n) & sel[:, n:n + 1])
        slope = slopes_ref[...]
        s_all = sc[...] * scale - slope * (past_len - kpos).astype(F32)
        s_all = jnp.where(chosen, s_all, MASKED)
        s_self = jnp.sum(qr * kn_ref[...], axis=-1, keepdims=True) * scale
        m = jnp.maximum(jnp.max(s_all, axis=-1, keepdims=True), s_self)
        p = jnp.exp(s_all - m)
        p_self = jnp.exp(s_self - m)
        p_scr[...] = p
        linv[...] = jnp.broadcast_to(1.0 / (jnp.sum(p, axis=-1, keepdims=True) + p_self), linv.shape)
        acc[...] = p_self * jnp.broadcast_to(vn_ref[...], (N_HEADS, D_MODEL))

    @pl.when(j >= k_steps)
    def _():
        part = None
        for i in range(group):
            page = (j - k_steps) * group + i
            pj = p_scr[:, pl.ds(pl.multiple_of(page * PAGE_SIZE, PAGE_SIZE), PAGE_SIZE)]
            d = _dot(pj.astype(BF16), _page_heads_on_lanes(v_refs[i]).astype(BF16))
            part = d if part is None else part + d
        acc[...] += part

    @pl.when(j == 2 * k_steps - 1)
    def _():
        a = acc[...] * linv[:, 0:1]
        for h in range(N_HEADS):
            hl = slice(h * HEAD_DIM, (h + 1) * HEAD_DIM)
            o_ref[:, hl] = a[h:h + 1, hl]


def _moba_sample(narrow3, k_new, v_new, cache_k, cache_v, page_table, group):
    n_seq, n_pages = page_table.shape
    past_len = n_pages * PAGE_SIZE
    assert past_len % MOBA_BLOCK == 0, "the decode token must start a MoBA block"
    assert n_pages % group == 0 and group % (MOBA_BLOCK // PAGE_SIZE) == 0
    n_blk = past_len // MOBA_BLOCK
    k_steps = n_pages // group
    page_rows = PAGE_SIZE * N_HEADS
    new_row = pl.BlockSpec((None, 1, D_MODEL), lambda i, j, pt: (i, 0, 0))

    def k_page(g):
        return pl.BlockSpec((page_rows, HEAD_DIM), lambda i, j, pt: (
            pt[i * n_pages + jnp.minimum(j, k_steps - 1) * group + g], 0))

    def v_page(g):
        def index(i, j, pt):
            in_k = j < k_steps
            seq_i = jnp.maximum(i - jnp.where(in_k, 1, 0), 0)
            grp = jnp.where(in_k, k_steps - 1, j - k_steps)
            return pt[seq_i * n_pages + grp * group + g], 0
        return pl.BlockSpec((page_rows, HEAD_DIM), index)

    grid_spec = pltpu.PrefetchScalarGridSpec(
        num_scalar_prefetch=1,
        grid=(n_seq, 2 * k_steps),
        in_specs=[pl.BlockSpec((N_HEADS, 1), lambda i, j, pt: (0, 0)),
                  pl.BlockSpec((None, None, 1, D_MODEL), lambda i, j, pt: (NARROW_MQ, i, 0, 0)), new_row, new_row]
                 + [k_page(g) for g in range(group)] + [v_page(g) for g in range(group)],
        out_specs=new_row,
        scratch_shapes=[pltpu.VMEM((N_HEADS, past_len), F32),
                        pltpu.VMEM((n_blk, D_MODEL), F32),
                        pltpu.VMEM((N_HEADS, past_len), F32),
                        pltpu.VMEM((N_HEADS, D_MODEL), F32),
                        pltpu.VMEM((N_HEADS, HEAD_DIM), F32)])
    return pl.pallas_call(
        functools.partial(_moba_sample_kernel, n_pages=n_pages, group=group),
        grid_spec=grid_spec,
        out_shape=jax.ShapeDtypeStruct((n_seq, 1, D_MODEL), F32),
        compiler_params=_params(2),
        name="moba_sample",
    )(page_table.reshape(-1), jnp.asarray(_alibi_slopes()).reshape(N_HEADS, 1),
      narrow3, k_new, v_new, *([cache_k] * group), *([cache_v] * group))


def _layer_norm(x, g, b):
    mu = jnp.mean(x, axis=-1, keepdims=True)
    xc = x - mu
    var = jnp.mean(xc * xc, axis=-1, keepdims=True)
    return xc * lax.rsqrt(var + LN_EPS) * g + b


def _mix_kernel(oa_ref, ob_ref, ga_ref, gb_ref, x_ref, wa_ref, wb_ref, wo_ref, g_ref, b_ref, h_ref):
    ya = _dot(oa_ref[...].astype(BF16), wa_ref[...])
    yb = _dot(ob_ref[...].astype(BF16), wb_ref[...])
    mixed = jax.nn.sigmoid(ga_ref[...].astype(F32)) * ya + jax.nn.sigmoid(gb_ref[...].astype(F32)) * yb
    mix = _dot(mixed.astype(BF16), wo_ref[...])
    h_ref[...] = _layer_norm(DEEPNORM_ALPHA * x_ref[...] + mix, g_ref[...], b_ref[...])


def _mix(o_a, o_b, narrow, x2d, wa, wb, wo, ln_g, ln_b, tm):
    m = x2d.shape[0]
    rows = pl.BlockSpec((tm, D_MODEL), lambda i: (i, 0))
    sec = lambda s: pl.BlockSpec((None, tm, D_MODEL), lambda i, s=s: (s, i, 0))
    full = lambda a: pl.BlockSpec(a.shape, lambda i: (0, 0))
    return pl.pallas_call(
        _mix_kernel,
        grid=(m // tm,),
        in_specs=[rows, rows, sec(NARROW_GA), sec(NARROW_GB), rows, full(wa), full(wb), full(wo), full(ln_g), full(ln_b)],
        out_specs=rows,
        out_shape=jax.ShapeDtypeStruct((m, D_MODEL), F32),
        compiler_params=_params(1),
        name="mix",
    )(o_a, o_b, narrow, narrow, x2d, wa, wb, wo, ln_g, ln_b)


def _mlp_kernel(h_ref, wu_ref, wd_ref, g_ref, b_ref, y_ref, acc):
    j = pl.program_id(1)
    h = h_ref[...]
    u = jnp.maximum(_dot(h.astype(BF16), wu_ref[...]), 0.0)
    part = _dot((u * u).astype(BF16), wd_ref[...])

    @pl.when(j == 0)
    def _():
        acc[...] = part

    @pl.when(j > 0)
    def _():
        acc[...] += part

    @pl.when(j == pl.num_programs(1) - 1)
    def _():
        y_ref[...] = _layer_norm(DEEPNORM_ALPHA * h + acc[...], g_ref[...], b_ref[...])


def _mlp(h, wu, wd, ln_g, ln_b, tm, tf):
    m = h.shape[0]
    d_ff = wu.shape[1]
    rows = pl.BlockSpec((tm, D_MODEL), lambda i, j: (i, 0))
    full = lambda a: pl.BlockSpec(a.shape, lambda i, j: (0, 0))
    return pl.pallas_call(
        _mlp_kernel,
        grid=(m // tm, d_ff // tf),
        in_specs=[rows,
                  pl.BlockSpec((D_MODEL, tf), lambda i, j: (0, j)),
                  pl.BlockSpec((tf, D_MODEL), lambda i, j: (j, 0)),
                  full(ln_g), full(ln_b)],
        out_specs=rows,
        out_shape=jax.ShapeDtypeStruct((m, D_MODEL), F32),
        scratch_shapes=[pltpu.VMEM((tm, D_MODEL), F32)],
        compiler_params=_params(2),
        name="mlp",
    )(h, wu, wd, ln_g, ln_b)


def kernel(x_prompt, x_sample, cache_k, cache_v, state_hgrn, page_table, w_in, lb_param, g_norm_a,
           w_proj_a, w_proj_b, w_out, ln1_g, ln1_b, w_up, w_down, ln2_g, ln2_b):
    assert w_in.shape[0] == DEPTH
    batch, seq, _ = x_prompt.shape
    n_seq, dec_seq, _ = x_sample.shape
    assert dec_seq == 1
    n_pool = cache_k.shape[1]

    w_in_b = w_in[0].astype(BF16)
    wa, wb, wo = w_proj_a[0].astype(BF16), w_proj_b[0].astype(BF16), w_out[0].astype(BF16)
    wu, wd = w_up[0].astype(BF16), w_down[0].astype(BF16)
    gn = g_norm_a[0].reshape(1, HEAD_DIM)
    g1, b1 = ln1_g[0].reshape(1, D_MODEL), ln1_b[0].reshape(1, D_MODEL)
    g2, b2 = ln2_g[0].reshape(1, D_MODEL), ln2_b[0].reshape(1, D_MODEL)

    xp = x_prompt.reshape(batch * seq, D_MODEL)
    wide_p, narrow_p, k_p, v_p = _inproj(xp, w_in_b, tm=1024)
    oa_p, st_p = _hgrn_prompt(wide_p, narrow_p, lb_param, gn, batch, seq, tt=512)
    ob_p = _moba_prompt(narrow_p, k_p, v_p, batch, seq)
    h_p = _mix(oa_p, ob_p, narrow_p, xp, wa, wb, wo, g1, b1, tm=512)
    y_p = _mlp(h_p, wu, wd, g2, b2, tm=1024, tf=1024)

    xs = x_sample.reshape(n_seq, D_MODEL)
    wide_s, narrow_s, k_s, v_s = _inproj(xs, w_in_b, tm=n_seq)
    narrow_s = narrow_s.astype(F32)
    oa_s, st_s = _hgrn_sample(wide_s, narrow_s, lb_param, gn, state_hgrn[0], sb=8)
    ob_s = _moba_sample(narrow_s.reshape(N_NARROW, n_seq, 1, D_MODEL),
                        k_s.reshape(n_seq, 1, D_MODEL), v_s.reshape(n_seq, 1, D_MODEL),
                        cache_k[0].reshape(n_pool * PAGE_SIZE * N_HEADS, HEAD_DIM),
                        cache_v[0].reshape(n_pool * PAGE_SIZE * N_HEADS, HEAD_DIM), page_table, group=16)
    h_s = _mix(oa_s, ob_s.reshape(n_seq, D_MODEL), narrow_s, xs, wa, wb, wo, g1, b1, tm=n_seq)
    y_s = _mlp(h_s, wu, wd, g2, b2, tm=n_seq, tf=1024)

    kv_p = (DEPTH, batch, seq, N_HEADS, HEAD_DIM)
    kv_s = (DEPTH, n_seq, dec_seq, N_HEADS, HEAD_DIM)
    return (y_p.reshape(batch, seq, D_MODEL), y_s.reshape(n_seq, dec_seq, D_MODEL),
            k_p.reshape(kv_p), v_p.reshape(kv_p), st_p[None],
            k_s.reshape(kv_s), v_s.reshape(kv_s), st_s[None])
```

```python
import functools

import numpy as np
import jax
import jax.numpy as jnp
from jax import lax
from jax.experimental import pallas as pl
from jax.experimental.pallas import tpu as pltpu

F32 = jnp.float32
BF16 = jnp.bfloat16

D_MODEL = 1024
N_HEADS = 8
HEAD_DIM = 128
N_COLS = 9
COL_MK, COL_MV = 5, 6
N_WIDE = 2
WIDE_Q, WIDE_F = range(N_WIDE)
N_NARROW = 5
NARROW_I, NARROW_G, NARROW_MQ, NARROW_GA, NARROW_GB = range(N_NARROW)
DEPTH = 1
LN_EPS = 1e-5
RMS_EPS = 1e-6
DEEPNORM_ALPHA = (2.0 * DEPTH) ** 0.25
MOBA_BLOCK = 256
MOBA_TOPK = 3
PAGE_SIZE = 128
MASKED = -2e30
M_INIT = -1e30
LOG2E = 1.4426950408889634
SUBLANES = 8
STRIP = 128
HGRN_CHUNK = 128
VMEM_LIMIT = 48 * 1024 * 1024

_NT = (((1,), (1,)), ((), ()))


def _dot(a, b):
    return jnp.dot(a, b, preferred_element_type=F32)


def _dot_nt(a, b, precision=None):
    return lax.dot_general(a, b, _NT, precision=precision, preferred_element_type=F32)


def _params(n_grid):
    return pltpu.CompilerParams(dimension_semantics=("arbitrary",) * n_grid,
                                vmem_limit_bytes=VMEM_LIMIT)


def _inproj_kernel(x_ref, w_ref, wide_ref, narrow_ref, k_ref, v_ref, xb):
    j = pl.program_id(1)

    @pl.when(j == 0)
    def _():
        xb[...] = x_ref[...].astype(BF16)

    y = _dot(xb[...], w_ref[...])

    @pl.when(j < N_WIDE)
    def _():
        wide_ref[...] = y

    @pl.when(j == COL_MK)
    def _():
        k_ref[...] = y

    @pl.when(j == COL_MV)
    def _():
        v_ref[...] = y

    @pl.when((j >= N_WIDE) & (j != COL_MK) & (j != COL_MV))
    def _():
        narrow_ref[...] = y.astype(narrow_ref.dtype)


def _inproj(x2d, w_bf16, tm):
    m = x2d.shape[0]

    def wide(i, j):
        return jnp.minimum(j, N_WIDE - 1), i, 0

    def narrow(i, j):
        sec = jnp.where(j < COL_MK, j - N_WIDE, jnp.where(j > COL_MV, j - N_WIDE - 2, COL_MK - 1 - N_WIDE))
        return jnp.maximum(sec, 0), i, 0

    rows = pl.BlockSpec((tm, D_MODEL), lambda i, j: (i, 0))
    return pl.pallas_call(
        _inproj_kernel,
        grid=(m // tm, N_COLS),
        in_specs=[rows, pl.BlockSpec((D_MODEL, D_MODEL), lambda i, j: (0, j))],
        out_specs=[pl.BlockSpec((None, tm, D_MODEL), wide), pl.BlockSpec((None, tm, D_MODEL), narrow), rows, rows],
        out_shape=[jax.ShapeDtypeStruct((N_WIDE, m, D_MODEL), F32),
                   jax.ShapeDtypeStruct((N_NARROW, m, D_MODEL), BF16),
                   jax.ShapeDtypeStruct((m, D_MODEL), F32),
                   jax.ShapeDtypeStruct((m, D_MODEL), F32)],
        scratch_shapes=[pltpu.VMEM((tm, D_MODEL), BF16)],
        compiler_params=_params(2),
        name="inproj",
    )(x2d, w_bf16)


def _lower_bound(p):
    m = p[0:1]
    for r in range(1, p.shape[0]):
        m = jnp.maximum(m, p[r:r + 1])
    e = [jnp.exp(p[r:r + 1] - m) for r in range(p.shape[0])]
    tot = e[0]
    for r in range(1, p.shape[0]):
        tot = tot + e[r]
    return e[0] / tot


def _gates_from_z(z, lb):
    e = jnp.exp(-jnp.abs(z))
    r = 1.0 / (1.0 + e)
    er = e * r
    pos = z >= 0
    oml = 1.0 - lb
    f = lb + oml * jnp.where(pos, r, er)
    k = oml * jnp.where(pos, er, r)
    return f, k


def _bc8(row):
    return jnp.broadcast_to(row, (SUBLANES, row.shape[-1]))


def _tile_levels(x, q, k, row_masks):
    ninf = -jnp.inf
    sides = []
    odd, upper_pair, upper_half = row_masks
    up = odd
    tot = pltpu.roll(x, 1, 0)
    sides.append((q * jnp.exp2(jnp.where(up, x, ninf)), jnp.where(up, 0.0, k)))
    sc = x + jnp.where(up, tot, 0.0)
    up = upper_pair
    tot = jnp.where(upper_half, _bc8(sc[5:6]), _bc8(sc[1:2]))
    sides.append((q * jnp.exp2(jnp.where(up, sc, ninf)), k * jnp.exp2(jnp.where(up, ninf, tot - sc))))
    sc = sc + jnp.where(up, tot, 0.0)
    up = upper_half
    tot = _bc8(sc[3:4])
    sides.append((q * jnp.exp2(jnp.where(up, sc, ninf)), k * jnp.exp2(jnp.where(up, ninf, tot - sc))))
    sc = sc + jnp.where(up, tot, 0.0)
    return sc, sides


def _hgrn_chunk(qa, z, v, ga, lb, gnorm, st):
    n_tiles = HGRN_CHUNK // SUBLANES
    f, k = _gates_from_z(z, lb)
    logf = jnp.log2(f)
    q = qa * jax.nn.sigmoid(qa)

    def tiles(a):
        return [a[SUBLANES * j:SUBLANES * (j + 1)] for j in range(n_tiles)]

    qt, kt = tiles(q), tiles(k)
    r8 = lax.broadcasted_iota(jnp.int32, (SUBLANES, HEAD_DIM), 0)
    row_masks = ((r8 & 1) == 1, (r8 & 2) == 2, r8 >= 4)
    in_tile = [_tile_levels(x, qt[j], kt[j], row_masks) for j, x in enumerate(tiles(logf))]
    levels = [(2 << b, jnp.concatenate([in_tile[j][1][b][0] for j in range(n_tiles)], axis=0),
               jnp.concatenate([in_tile[j][1][b][1] for j in range(n_tiles)], axis=0)) for b in range(3)]

    sc = [t[0] for t in in_tile]
    zero = jnp.zeros((SUBLANES, HEAD_DIM), F32)
    nb = 1
    while nb < n_tiles:
        ql = [zero] * n_tiles
        kl = [zero] * n_tiles
        new = list(sc)
        for m in range(n_tiles // (2 * nb)):
            lo = range(2 * m * nb, (2 * m + 1) * nb)
            up = range((2 * m + 1) * nb, (2 * m + 2) * nb)
            tot = _bc8(sc[lo[-1]][SUBLANES - 1:SUBLANES])
            for j in lo:
                kl[j] = kt[j] * jnp.exp2(tot - sc[j])
            for j in up:
                ql[j] = qt[j] * jnp.exp2(sc[j])
                new[j] = sc[j] + tot
        levels.append((2 * nb * SUBLANES, jnp.concatenate(ql, axis=0), jnp.concatenate(kl, axis=0)))
        sc = new
        nb *= 2
    a_t = sc
    a_last = a_t[n_tiles - 1][SUBLANES - 1:SUBLANES]

    row = lax.broadcasted_iota(jnp.int32, (HGRN_CHUNK, HGRN_CHUNK), 0)
    col = lax.broadcasted_iota(jnp.int32, (HGRN_CHUNK, HGRN_CHUNK), 1)
    differ = row ^ col
    scores = jnp.where(differ == 0, jnp.sum(q * k, axis=-1, keepdims=True), 0.0)
    for group, ql, kl in levels:
        x = _dot_nt(ql.astype(BF16), kl.astype(BF16))
        if group < HGRN_CHUNK:
            x = jnp.where(differ < group, x, 0.0)
        scores = scores + x

    qg = jnp.concatenate([qt[j] * jnp.exp2(a_t[j]) for j in range(n_tiles)], axis=0)
    kd = jnp.concatenate([kt[j] * jnp.exp2(_bc8(a_last) - a_t[j]) for j in range(n_tiles)], axis=0)
    o = _dot_nt(qg.astype(BF16), st.astype(BF16)) + _dot(scores.astype(BF16), v.astype(BF16))
    st_new = st * jnp.exp2(a_last) + _dot(v.T.astype(BF16), kd.astype(BF16))

    ms = jnp.mean(o * o, axis=-1, keepdims=True)
    o = o * lax.rsqrt(ms + RMS_EPS) * gnorm * (ga * jax.nn.sigmoid(ga))
    return o, st_new


def _hgrn_prompt_kernel(q_ref, f_ref, i_ref, g_ref, lbp_ref, gn_ref, o_ref, s_ref, st_scr, *, n_chunks):
    t = pl.program_id(1)

    @pl.when(t == 0)
    def _():
        st_scr[...] = jnp.zeros(st_scr.shape, F32)

    gnorm = gn_ref[...]

    def chunk_body(c, carry):
        rows = pl.ds(pl.multiple_of(c * HGRN_CHUNK, HGRN_CHUNK), HGRN_CHUNK)

        for h in range(N_HEADS):
            lanes = slice(h * HEAD_DIM, (h + 1) * HEAD_DIM)
            lb = _lower_bound(lbp_ref[:, lanes])
            o, st_new = _hgrn_chunk(q_ref[rows, lanes], f_ref[rows, lanes], i_ref[rows, lanes].astype(F32),
                                    g_ref[rows, lanes].astype(F32), lb, gnorm, st_scr[h])
            st_scr[h] = st_new
            o_ref[rows, lanes] = o.astype(o_ref.dtype)
        return carry

    lax.fori_loop(0, n_chunks, chunk_body, 0)

    @pl.when(t == pl.num_programs(1) - 1)
    def _():
        for h in range(N_HEADS):
            s_ref[h] = st_scr[h].T


def _hgrn_prompt(wide, narrow, lb_param, g_norm, batch, seq, tt):
    n_t = seq // tt

    def sec(s):
        return pl.BlockSpec((None, tt, D_MODEL), lambda b, t, s=s: (s, b * n_t + t, 0))

    return pl.pallas_call(
        functools.partial(_hgrn_prompt_kernel, n_chunks=tt // HGRN_CHUNK),
        grid=(batch, n_t),
        in_specs=[sec(WIDE_Q), sec(WIDE_F), sec(NARROW_I), sec(NARROW_G),
                  pl.BlockSpec(lb_param.shape, lambda b, t: (0, 0)),
                  pl.BlockSpec(g_norm.shape, lambda b, t: (0, 0))],
        out_specs=[pl.BlockSpec((tt, D_MODEL), lambda b, t: (b * n_t + t, 0)),
                   pl.BlockSpec((None, N_HEADS, HEAD_DIM, HEAD_DIM), lambda b, t: (b, 0, 0, 0))],
        out_shape=[jax.ShapeDtypeStruct((batch * seq, D_MODEL), BF16),
                   jax.ShapeDtypeStruct((batch, N_HEADS, HEAD_DIM, HEAD_DIM), F32)],
        scratch_shapes=[pltpu.VMEM((N_HEADS, HEAD_DIM, HEAD_DIM), F32)],
        compiler_params=_params(2),
        name="hgrn_prompt",
    )(wide, wide, narrow, narrow, lb_param, g_norm)


def _hgrn_sample_kernel(q_ref, f_ref, i_ref, g_ref, lbp_ref, gn_ref, s0_ref, o_ref, s_ref):
    sb = q_ref.shape[0]
    lb = _lower_bound(lbp_ref[...])
    f, k = _gates_from_z(f_ref[...], lb)
    qa = q_ref[...]
    q = qa * jax.nn.sigmoid(qa)
    ga = g_ref[...]
    gate = ga * jax.nn.sigmoid(ga)
    v = i_ref[...]
    gnorm = gn_ref[...]
    qf = q * f
    qk = q * k
    for h in range(N_HEADS):
        hl = slice(h * HEAD_DIM, (h + 1) * HEAD_DIM)
        f_cols = f[:, hl].T
        k_cols = k[:, hl].T
        for b in range(sb):
            s0 = s0_ref[b, h]
            v_row = v[b:b + 1, hl]
            s_ref[b, h] = s0 * f_cols[:, b:b + 1] + k_cols[:, b:b + 1] * v_row
            o = _dot(qf[b:b + 1, hl].astype(BF16), s0.astype(BF16)) \
                + jnp.sum(qk[b:b + 1, hl], axis=-1, keepdims=True) * v_row
            ms = jnp.mean(o * o, axis=-1, keepdims=True)
            o_ref[b:b + 1, hl] = o * lax.rsqrt(ms + RMS_EPS) * gnorm * gate[b:b + 1, hl]


def _hgrn_sample(wide, narrow, lb_param, g_norm, state, sb):
    n = state.shape[0]

    def sec(s):
        return pl.BlockSpec((None, sb, D_MODEL), lambda i, s=s: (s, i, 0))

    st_spec = pl.BlockSpec((sb, N_HEADS, HEAD_DIM, HEAD_DIM), lambda i: (i, 0, 0, 0))
    return pl.pallas_call(
        _hgrn_sample_kernel,
        grid=(n // sb,),
        in_specs=[sec(WIDE_Q), sec(WIDE_F), sec(NARROW_I), sec(NARROW_G),
                  pl.BlockSpec(lb_param.shape, lambda i: (0, 0)),
                  pl.BlockSpec(g_norm.shape, lambda i: (0, 0)),
                  st_spec],
        out_specs=[pl.BlockSpec((sb, D_MODEL), lambda i: (i, 0)), st_spec],
        out_shape=[jax.ShapeDtypeStruct((n, D_MODEL), F32),
                   jax.ShapeDtypeStruct(state.shape, F32)],
        compiler_params=_params(1),
        name="hgrn_sample",
    )(wide, wide, narrow, narrow, lb_param, g_norm, state)


def _alibi_slopes():
    return np.asarray([2.0 ** (-8.0 * (h + 1) / N_HEADS) for h in range(N_HEADS)], np.float32)


def _topk_rank(g, idx, n):
    axis, count = n
    rank = jnp.zeros(g.shape, F32)
    for m in range(count):
        gm = lax.slice_in_dim(g, m, m + 1, axis=axis)
        rank = rank + ((gm > g) | ((gm == g) & (idx > m))).astype(F32)
    return rank


def _moba_prompt_kernel(slopes_ref, q_ref, k_ref, v_ref, o_ref,
                        kb, vt, qtb, kmean, bias, dist, *, n_blk):
    h = pl.program_id(1)
    blk = MOBA_BLOCK
    n_pairs = n_blk // 2
    slope2 = slopes_ref[h] * LOG2E
    scale2 = HEAD_DIM ** -0.5 * LOG2E

    def blk_rows(n):
        return slice(n * blk, (n + 1) * blk)

    def prep(n, carry):
        rows = blk_rows(n)
        kk = k_ref[rows, :]
        kb[rows, :] = kk.astype(BF16)
        kmean[pl.ds(n, 1), :] = jnp.sum(kk, axis=0, keepdims=True) * (1.0 / blk)
        vt[:, rows] = v_ref[rows, :].T.astype(BF16)
        return carry

    for n in range(n_blk):
        prep(n, 0)

    key = lax.broadcasted_iota(jnp.int32, (blk, blk), 0)
    qry = lax.broadcasted_iota(jnp.int32, (blk, blk), 1)
    d0 = slope2 * (qry - key).astype(F32)
    dist[0] = d0
    dist[1] = jnp.where(key <= qry, d0, -MASKED)

    def choose(qi, carry):
        rows = blk_rows(qi)
        q_t = q_ref[rows, :].astype(F32).T
        qtb[:, rows] = (q_t * scale2).astype(BF16)
        gate = jnp.dot(kmean[...], q_t, precision=lax.Precision.HIGHEST, preferred_element_type=F32)
        nio = lax.broadcasted_iota(jnp.int32, gate.shape, 0)
        past = nio < qi
        g = jnp.where(past, gate, -jnp.inf)
        attend = (past & (_topk_rank(g, nio, (0, n_blk)) < MOBA_TOPK)) | (nio == qi)
        bias[:, rows] = jnp.where(attend, -slope2 * ((qi - nio) * blk).astype(F32), MASKED)
        return carry

    for qi in range(n_blk):
        choose(qi, 0)

    for j in range(n_pairs):
        cols = slice(j * 2 * blk, (j + 1) * 2 * blk)
        m = jnp.full((1, 2 * blk), M_INIT, F32)
        l = jnp.zeros((1, 2 * blk), F32)
        acc = jnp.zeros((HEAD_DIM, 2 * blk), F32)
        for i in range(j + 1):
            keys = slice(i * 2 * blk, (i + 1) * 2 * blk)
            s = _dot(kb[keys, :], qtb[:, cols])
            own = 1 if i == j else 0
            ms, ls, alphas, ps = [], [], [], []
            for t in range(2 * blk // STRIP):
                lanes = slice(t * STRIP, (t + 1) * STRIP)
                in_blk = slice(t * STRIP % blk, t * STRIP % blk + STRIP)
                first = t * STRIP < blk
                xa = s[:blk, lanes] - dist[own if first else 0, :, in_blk]
                xb = s[blk:, lanes] - dist[0 if first else own, :, in_blk]
                ba = bias[2 * i:2 * i + 1, j * 2 * blk + t * STRIP:j * 2 * blk + (t + 1) * STRIP]
                bb = bias[2 * i + 1:2 * i + 2, j * 2 * blk + t * STRIP:j * 2 * blk + (t + 1) * STRIP]
                m_t = jnp.maximum(m[:, lanes], jnp.maximum(jnp.max(xa, axis=0, keepdims=True) + ba,
                                                           jnp.max(xb, axis=0, keepdims=True) + bb))
                alpha_t = jnp.exp2(m[:, lanes] - m_t)
                pa = jnp.exp2(xa - (m_t - ba))
                pb = jnp.exp2(xb - (m_t - bb))
                ls.append(alpha_t * l[:, lanes] + jnp.sum(pa, axis=0, keepdims=True)
                          + jnp.sum(pb, axis=0, keepdims=True))
                ms.append(m_t)
                alphas.append(alpha_t)
                ps.append(jnp.concatenate([pa.astype(BF16), pb.astype(BF16)], axis=0))
            m = jnp.concatenate(ms, axis=1)
            l = jnp.concatenate(ls, axis=1)
            acc = acc * jnp.concatenate(alphas, axis=1) + _dot(vt[:, keys], jnp.concatenate(ps, axis=1))
        o_ref[cols, :] = (acc * (1.0 / l)).T.astype(o_ref.dtype)


def _moba_prompt(narrow, k_rows, v_rows, batch, seq):
    n_blk = seq // MOBA_BLOCK
    assert n_blk % 2 == 0, "key blocks are processed in pairs"
    head = pl.BlockSpec((seq, HEAD_DIM), lambda b, h: (b, h))
    return pl.pallas_call(
        functools.partial(_moba_prompt_kernel, n_blk=n_blk),
        grid=(batch, N_HEADS),
        in_specs=[pl.BlockSpec(memory_space=pltpu.SMEM),
                  pl.BlockSpec((None, seq, HEAD_DIM), lambda b, h: (NARROW_MQ, b, h)),
                  head, head],
        out_specs=head,
        out_shape=jax.ShapeDtypeStruct((batch * seq, D_MODEL), BF16),
        scratch_shapes=[pltpu.VMEM((seq, HEAD_DIM), BF16),
                        pltpu.VMEM((HEAD_DIM, seq), BF16),
                        pltpu.VMEM((HEAD_DIM, seq), BF16),
                        pltpu.VMEM((n_blk, HEAD_DIM), F32),
                        pltpu.VMEM((n_blk, seq), F32),
                        pltpu.VMEM((2, MOBA_BLOCK, MOBA_BLOCK), F32)],
        compiler_params=_params(2),
        name="moba_prompt",
    )(jnp.asarray(_alibi_slopes()), narrow, k_rows, v_rows)


def _page_heads_on_lanes(page_ref):
    return jnp.concatenate([page_ref[pl.ds(h, PAGE_SIZE, stride=N_HEADS), :] for h in range(N_HEADS)], axis=1)


def _moba_sample_kernel(pt_ref, slopes_ref, q_ref, kn_ref, vn_ref, *rest, n_pages, group):
    del pt_ref
    k_refs, v_refs = rest[:group], rest[group:2 * group]
    o_ref, sc, ksum, p_scr, acc, linv = rest[2 * group:]
    j = pl.program_id(1)
    k_steps = n_pages // group
    past_len = n_pages * PAGE_SIZE
    n_blk = past_len // MOBA_BLOCK
    pages_per_blk = MOBA_BLOCK // PAGE_SIZE
    blks_per_step = group // pages_per_blk
    scale = HEAD_DIM ** -0.5
    q = q_ref[...].astype(F32)
    head_of_lane = lax.shift_right_logical(
        lax.broadcasted_iota(jnp.int32, (N_HEADS, D_MODEL), 1), HEAD_DIM.bit_length() - 1)
    head_of_row = lax.broadcasted_iota(jnp.int32, (N_HEADS, D_MODEL), 0)
    own_lanes = head_of_lane == head_of_row
    qr = jnp.where(own_lanes, jnp.broadcast_to(q, (N_HEADS, D_MODEL)), 0.0)

    @pl.when(j < k_steps)
    def _():
        qrb = qr.astype(BF16)
        sums = []
        for i in range(group):
            kp = _page_heads_on_lanes(k_refs[i])
            page = j * group + i
            sc[:, pl.ds(pl.multiple_of(page * PAGE_SIZE, PAGE_SIZE), PAGE_SIZE)] = _dot_nt(qrb, kp.astype(BF16))
            sums.append(jnp.sum(kp, axis=0, keepdims=True))
        for b in range(blks_per_step):
            tot = sums[b * pages_per_blk]
            for r in range(1, pages_per_blk):
                tot = tot + sums[b * pages_per_blk + r]
            ksum[pl.ds(j * blks_per_step + b, 1), :] = tot

    @pl.when(j == k_steps - 1)
    def _():
        kmean = ksum[...] * (1.0 / MOBA_BLOCK)
        gate = _dot_nt(qr, kmean, precision=lax.Precision.HIGHEST)
        nio = lax.broadcasted_iota(jnp.int32, gate.shape, 1)
        sel = _topk_rank(gate, nio, (1, n_blk)) < MOBA_TOPK
        kpos = lax.broadcasted_iota(jnp.int32, (N_HEADS, past_len), 1)
        blk_of_key = lax.shift_right_logical(kpos, MOBA_BLOCK.bit_length() - 1)
        chosen = jnp.zeros((N_HEADS, past_len), jnp.bool_)
        for n in range(n_blk):
            chosen = chosen | ((blk_of_key == n) & sel[:, n:n + 1])
        slope = slopes_ref[...]
        s_all = sc[...] * scale - slope * (past_len - kpos).astype(F32)
        s_all = jnp.where(chosen, s_all, MASKED)
        s_self = jnp.sum(qr * kn_ref[...], axis=-1, keepdims=True) * scale
        m = jnp.maximum(jnp.max(s_all, axis=-1, keepdims=True), s_self)
        p = jnp.exp(s_all - m)
        p_self = jnp.exp(s_self - m)
        p_scr[...] = p
        linv[...] = jnp.broadcast_to(1.0 / (jnp.sum(p, axis=-1, keepdims=True) + p_self), linv.shape)
        acc[...] = p_self * jnp.broadcast_to(vn_ref[...], (N_HEADS, D_MODEL))

    @pl.when(j >= k_steps)
    def _():
        part = None
        for i in range(group):
            page = (j - k_steps) * group + i
            pj = p_scr[:, pl.ds(pl.multiple_of(page * PAGE_SIZE, PAGE_SIZE), PAGE_SIZE)]
            d = _dot(pj.astype(BF16), _page_heads_on_lanes(v_refs[i]).astype(BF16))
            part = d if part is None else part + d
        acc[...] += part

    @pl.when(j == 2 * k_steps - 1)
    def _():
        a = acc[...] * linv[:, 0:1]
        for h in range(N_HEADS):
            hl = slice(h * HEAD_DIM, (h + 1) * HEAD_DIM)
            o_ref[:, hl] = a[h:h + 1, hl]


def _moba_sample(narrow3, k_new, v_new, cache_k, cache_v, page_table, group):
    n_seq, n_pages = page_table.shape
    past_len = n_pages * PAGE_SIZE
    assert past_len % MOBA_BLOCK == 0, "the decode token must start a MoBA block"
    assert n_pages % group == 0 and group % (MOBA_BLOCK // PAGE_SIZE) == 0
    n_blk = past_len // MOBA_BLOCK
    k_steps = n_pages // group
    page_rows = PAGE_SIZE * N_HEADS
    new_row = pl.BlockSpec((None, 1, D_MODEL), lambda i, j, pt: (i, 0, 0))

    def k_page(g):
        return pl.BlockSpec((page_rows, HEAD_DIM), lambda i, j, pt: (
            pt[i * n_pages + jnp.minimum(j, k_steps - 1) * group + g], 0))

    def v_page(g):
        def index(i, j, pt):
            in_k = j < k_steps
            seq_i = jnp.maximum(i - jnp.where(in_k, 1, 0), 0)
            grp = jnp.where(in_k, k_steps - 1, j - k_steps)
            return pt[seq_i * n_pages + grp * group + g], 0
        return pl.BlockSpec((page_rows, HEAD_DIM), index)

    grid_spec = pltpu.PrefetchScalarGridSpec(
        num_scalar_prefetch=1,
        grid=(n_seq, 2 * k_steps),
        in_specs=[pl.BlockSpec((N_HEADS, 1), lambda i, j, pt: (0, 0)),
                  pl.BlockSpec((None, None, 1, D_MODEL), lambda i, j, pt: (NARROW_MQ, i, 0, 0)), new_row, new_row]
                 + [k_page(g) for g in range(group)] + [v_page(g) for g in range(group)],
        out_specs=new_row,
        scratch_shapes=[pltpu.VMEM((N_HEADS, past_len), F32),
                        pltpu.VMEM((n_blk, D_MODEL), F32),
                        pltpu.VMEM((N_HEADS, past_len), F32),
                        pltpu.VMEM((N_HEADS, D_MODEL), F32),
                        pltpu.VMEM((N_HEADS, HEAD_DIM), F32)])
    return pl.pallas_call(
        functools.partial(_moba_sample_kernel, n_pages=n_pages, group=group),
        grid_spec=grid_spec,
        out_shape=jax.ShapeDtypeStruct((n_seq, 1, D_MODEL), F32),
        compiler_params=_params(2),
        name="moba_sample",
    )(page_table.reshape(-1), jnp.asarray(_alibi_slopes()).reshape(N_HEADS, 1),
      narrow3, k_new, v_new, *([cache_k] * group), *([cache_v] * group))


def _layer_norm(x, g, b):
    mu = jnp.mean(x, axis=-1, keepdims=True)
    xc = x - mu
    var = jnp.mean(xc * xc, axis=-1, keepdims=True)
    return xc * lax.rsqrt(var + LN_EPS) * g + b


def _mix_kernel(oa_ref, ob_ref, ga_ref, gb_ref, x_ref, wa_ref, wb_ref, wo_ref, g_ref, b_ref, h_ref):
    ya = _dot(oa_ref[...].astype(BF16), wa_ref[...])
    yb = _dot(ob_ref[...].astype(BF16), wb_ref[...])
    mixed = jax.nn.sigmoid(ga_ref[...].astype(F32)) * ya + jax.nn.sigmoid(gb_ref[...].astype(F32)) * yb
    mix = _dot(mixed.astype(BF16), wo_ref[...])
    h_ref[...] = _layer_norm(DEEPNORM_ALPHA * x_ref[...] + mix, g_ref[...], b_ref[...])


def _mix(o_a, o_b, narrow, x2d, wa, wb, wo, ln_g, ln_b, tm):
    m = x2d.shape[0]
    rows = pl.BlockSpec((tm, D_MODEL), lambda i: (i, 0))
    sec = lambda s: pl.BlockSpec((None, tm, D_MODEL), lambda i, s=s: (s, i, 0))
    full = lambda a: pl.BlockSpec(a.shape, lambda i: (0, 0))
    return pl.pallas_call(
        _mix_kernel,
        grid=(m // tm,),
        in_specs=[rows, rows, sec(NARROW_GA), sec(NARROW_GB), rows, full(wa), full(wb), full(wo), full(ln_g), full(ln_b)],
        out_specs=rows,
        out_shape=jax.ShapeDtypeStruct((m, D_MODEL), F32),
        compiler_params=_params(1),
        name="mix",
    )(o_a, o_b, narrow, narrow, x2d, wa, wb, wo, ln_g, ln_b)


def _mlp_kernel(h_ref, wu_ref, wd_ref, g_ref, b_ref, y_ref, acc, hb):
    j = pl.program_id(1)

    @pl.when(j == 0)
    def _():
        hb[...] = h_ref[...].astype(BF16)
        acc[...] = jnp.zeros(acc.shape, F32)

    u = jnp.maximum(_dot(hb[...], wu_ref[...]), 0.0)
    acc[...] += _dot((u * u).astype(BF16), wd_ref[...])

    @pl.when(j == pl.num_programs(1) - 1)
    def _():
        y_ref[...] = _layer_norm(DEEPNORM_ALPHA * h_ref[...] + acc[...], g_ref[...], b_ref[...])


def _mlp(h, wu, wd, ln_g, ln_b, tm, tf):
    m = h.shape[0]
    d_ff = wu.shape[1]
    rows = pl.BlockSpec((tm, D_MODEL), lambda i, j: (i, 0))
    full = lambda a: pl.BlockSpec(a.shape, lambda i, j: (0, 0))
    return pl.pallas_call(
        _mlp_kernel,
        grid=(m // tm, d_ff // tf),
        in_specs=[rows,
                  pl.BlockSpec((D_MODEL, tf), lambda i, j: (0, j)),
                  pl.BlockSpec((tf, D_MODEL), lambda i, j: (j, 0)),
                  full(ln_g), full(ln_b)],
        out_specs=rows,
        out_shape=jax.ShapeDtypeStruct((m, D_MODEL), F32),
        scratch_shapes=[pltpu.VMEM((tm, D_MODEL), F32), pltpu.VMEM((tm, D_MODEL), BF16)],
        compiler_params=_params(2),
        name="mlp",
    )(h, wu, wd, ln_g, ln_b)


def kernel(x_prompt, x_sample, cache_k, cache_v, state_hgrn, page_table, w_in, lb_param, g_norm_a,
           w_proj_a, w_proj_b, w_out, ln1_g, ln1_b, w_up, w_down, ln2_g, ln2_b):
    assert w_in.shape[0] == DEPTH
    batch, seq, _ = x_prompt.shape
    n_seq, dec_seq, _ = x_sample.shape
    assert dec_seq == 1
    n_pool = cache_k.shape[1]

    w_in_b = w_in[0].astype(BF16)
    wa, wb, wo = w_proj_a[0].astype(BF16), w_proj_b[0].astype(BF16), w_out[0].astype(BF16)
    wu, wd = w_up[0].astype(BF16), w_down[0].astype(BF16)
    gn = g_norm_a[0].reshape(1, HEAD_DIM)
    g1, b1 = ln1_g[0].reshape(1, D_MODEL), ln1_b[0].reshape(1, D_MODEL)
    g2, b2 = ln2_g[0].reshape(1, D_MODEL), ln2_b[0].reshape(1, D_MODEL)

    xp = x_prompt.reshape(batch * seq, D_MODEL)
    wide_p, narrow_p, k_p, v_p = _inproj(xp, w_in_b, tm=1024)
    oa_p, st_p = _hgrn_prompt(wide_p, narrow_p, lb_param, gn, batch, seq, tt=512)
    ob_p = _moba_prompt(narrow_p, k_p, v_p, batch, seq)
    h_p = _mix(oa_p, ob_p, narrow_p, xp, wa, wb, wo, g1, b1, tm=512)
    y_p = _mlp(h_p, wu, wd, g2, b2, tm=1024, tf=1024)

    xs = x_sample.reshape(n_seq, D_MODEL)
    wide_s, narrow_s, k_s, v_s = _inproj(xs, w_in_b, tm=n_seq)
    narrow_s = narrow_s.astype(F32)
    oa_s, st_s = _hgrn_sample(wide_s, narrow_s, lb_param, gn, state_hgrn[0], sb=8)
    ob_s = _moba_sample(narrow_s.reshape(N_NARROW, n_seq, 1, D_MODEL),
                        k_s.reshape(n_seq, 1, D_MODEL), v_s.reshape(n_seq, 1, D_MODEL),
                        cache_k[0].reshape(n_pool * PAGE_SIZE * N_HEADS, HEAD_DIM),
                        cache_v[0].reshape(n_pool * PAGE_SIZE * N_HEADS, HEAD_DIM), page_table, group=16)
    h_s = _mix(oa_s, ob_s.reshape(n_seq, D_MODEL), narrow_s, xs, wa, wb, wo, g1, b1, tm=n_seq)
    y_s = _mlp(h_s, wu, wd, g2, b2, tm=n_seq, tf=1024)

    kv_p = (DEPTH, batch, seq, N_HEADS, HEAD_DIM)
    kv_s = (DEPTH, n_seq, dec_seq, N_HEADS, HEAD_DIM)
    return (y_p.reshape(batch, seq, D_MODEL), y_s.reshape(n_seq, dec_seq, D_MODEL),
            k_p.reshape(kv_p), v_p.reshape(kv_p), st_p[None],
            k_s.reshape(kv_s), v_s.reshape(kv_s), st_s[None])
```

```python
import functools

import numpy as np
import jax
import jax.numpy as jnp
from jax import lax
from jax.experimental import pallas as pl
from jax.experimental.pallas import tpu as pltpu

F32 = jnp.float32
BF16 = jnp.bfloat16

D_MODEL = 1024
N_HEADS = 8
HEAD_DIM = 128
N_COLS = 9
COL_MK, COL_MV = 5, 6
N_WIDE = 2
WIDE_Q, WIDE_F = range(N_WIDE)
N_NARROW = 5
NARROW_I, NARROW_G, NARROW_MQ, NARROW_GA, NARROW_GB = range(N_NARROW)
DEPTH = 1
LN_EPS = 1e-5
RMS_EPS = 1e-6
DEEPNORM_ALPHA = (2.0 * DEPTH) ** 0.25
MOBA_BLOCK = 256
MOBA_TOPK = 3
PAGE_SIZE = 128
MASKED = -2e30
M_INIT = -1e30
LOG2E = 1.4426950408889634
SUBLANES = 8
STRIP = 128
HGRN_CHUNK = 128
VMEM_LIMIT = 48 * 1024 * 1024

_NT = (((1,), (1,)), ((), ()))


def _dot(a, b):
    return jnp.dot(a, b, preferred_element_type=F32)


def _dot_nt(a, b, precision=None):
    return lax.dot_general(a, b, _NT, precision=precision, preferred_element_type=F32)


def _params(n_grid):
    return pltpu.CompilerParams(dimension_semantics=("arbitrary",) * n_grid,
                                vmem_limit_bytes=VMEM_LIMIT)


def _inproj_kernel(x_ref, w_ref, wide_ref, narrow_ref, k_ref, v_ref, xb):
    j = pl.program_id(1)

    @pl.when(j == 0)
    def _():
        xb[...] = x_ref[...].astype(BF16)

    y = _dot(xb[...], w_ref[...])

    @pl.when(j < N_WIDE)
    def _():
        wide_ref[...] = y

    @pl.when(j == COL_MK)
    def _():
        k_ref[...] = y

    @pl.when(j == COL_MV)
    def _():
        v_ref[...] = y

    @pl.when((j >= N_WIDE) & (j != COL_MK) & (j != COL_MV))
    def _():
        narrow_ref[...] = y.astype(narrow_ref.dtype)


def _inproj(x2d, w_bf16, tm):
    m = x2d.shape[0]

    def wide(i, j):
        return jnp.minimum(j, N_WIDE - 1), i, 0

    def narrow(i, j):
        sec = jnp.where(j < COL_MK, j - N_WIDE, jnp.where(j > COL_MV, j - N_WIDE - 2, COL_MK - 1 - N_WIDE))
        return jnp.maximum(sec, 0), i, 0

    rows = pl.BlockSpec((tm, D_MODEL), lambda i, j: (i, 0))
    return pl.pallas_call(
        _inproj_kernel,
        grid=(m // tm, N_COLS),
        in_specs=[rows, pl.BlockSpec((D_MODEL, D_MODEL), lambda i, j: (0, j))],
        out_specs=[pl.BlockSpec((None, tm, D_MODEL), wide), pl.BlockSpec((None, tm, D_MODEL), narrow), rows, rows],
        out_shape=[jax.ShapeDtypeStruct((N_WIDE, m, D_MODEL), F32),
                   jax.ShapeDtypeStruct((N_NARROW, m, D_MODEL), BF16),
                   jax.ShapeDtypeStruct((m, D_MODEL), F32),
                   jax.ShapeDtypeStruct((m, D_MODEL), F32)],
        scratch_shapes=[pltpu.VMEM((tm, D_MODEL), BF16)],
        compiler_params=_params(2),
        name="inproj",
    )(x2d, w_bf16)


def _lower_bound(p):
    m = p[0:1]
    for r in range(1, p.shape[0]):
        m = jnp.maximum(m, p[r:r + 1])
    e = [jnp.exp(p[r:r + 1] - m) for r in range(p.shape[0])]
    tot = e[0]
    for r in range(1, p.shape[0]):
        tot = tot + e[r]
    return e[0] / tot


def _gates_from_z(z, lb):
    e = jnp.exp(-jnp.abs(z))
    r = 1.0 / (1.0 + e)
    er = e * r
    pos = z >= 0
    oml = 1.0 - lb
    f = lb + oml * jnp.where(pos, r, er)
    k = oml * jnp.where(pos, er, r)
    return f, k


def _bc8(row):
    return jnp.broadcast_to(row, (SUBLANES, row.shape[-1]))


def _tile_levels(x, q, k, row_masks):
    ninf = -jnp.inf
    sides = []
    odd, upper_pair, upper_half = row_masks
    up = odd
    tot = pltpu.roll(x, 1, 0)
    sides.append((q * jnp.exp2(jnp.where(up, x, ninf)), jnp.where(up, 0.0, k)))
    sc = x + jnp.where(up, tot, 0.0)
    up = upper_pair
    tot = jnp.where(upper_half, _bc8(sc[5:6]), _bc8(sc[1:2]))
    sides.append((q * jnp.exp2(jnp.where(up, sc, ninf)), k * jnp.exp2(jnp.where(up, ninf, tot - sc))))
    sc = sc + jnp.where(up, tot, 0.0)
    up = upper_half
    tot = _bc8(sc[3:4])
    sides.append((q * jnp.exp2(jnp.where(up, sc, ninf)), k * jnp.exp2(jnp.where(up, ninf, tot - sc))))
    sc = sc + jnp.where(up, tot, 0.0)
    return sc, sides


def _hgrn_sides(qa, z, lb):
    n_tiles = HGRN_CHUNK // SUBLANES
    f, k = _gates_from_z(z, lb)
    logf = jnp.log2(f)
    q = qa * jax.nn.sigmoid(qa)

    def tiles(a):
        return [a[SUBLANES * j:SUBLANES * (j + 1)] for j in range(n_tiles)]

    qt, kt = tiles(q), tiles(k)
    r8 = lax.broadcasted_iota(jnp.int32, (SUBLANES, HEAD_DIM), 0)
    row_masks = ((r8 & 1) == 1, (r8 & 2) == 2, r8 >= 4)
    in_tile = [_tile_levels(x, qt[j], kt[j], row_masks) for j, x in enumerate(tiles(logf))]
    levels = [(2 << b, jnp.concatenate([in_tile[j][1][b][0] for j in range(n_tiles)], axis=0),
               jnp.concatenate([in_tile[j][1][b][1] for j in range(n_tiles)], axis=0)) for b in range(3)]

    sc = [t[0] for t in in_tile]
    zero = jnp.zeros((SUBLANES, HEAD_DIM), F32)
    nb = 1
    while nb < n_tiles:
        ql = [zero] * n_tiles
        kl = [zero] * n_tiles
        new = list(sc)
        for m in range(n_tiles // (2 * nb)):
            lo = range(2 * m * nb, (2 * m + 1) * nb)
            up = range((2 * m + 1) * nb, (2 * m + 2) * nb)
            tot = _bc8(sc[lo[-1]][SUBLANES - 1:SUBLANES])
            for j in lo:
                kl[j] = kt[j] * jnp.exp2(tot - sc[j])
            for j in up:
                ql[j] = qt[j] * jnp.exp2(sc[j])
                new[j] = sc[j] + tot
        levels.append((2 * nb * SUBLANES, jnp.concatenate(ql, axis=0), jnp.concatenate(kl, axis=0)))
        sc = new
        nb *= 2
    a_t = sc
    a_last = a_t[n_tiles - 1][SUBLANES - 1:SUBLANES]
    qg = jnp.concatenate([qt[j] * jnp.exp2(a_t[j]) for j in range(n_tiles)], axis=0)
    kd = jnp.concatenate([kt[j] * jnp.exp2(_bc8(a_last) - a_t[j]) for j in range(n_tiles)], axis=0)
    diag = jnp.sum(q * k, axis=-1, keepdims=True)
    return ([(group, ql.astype(BF16), kl.astype(BF16)) for group, ql, kl in levels],
            diag, qg.astype(BF16), kd.astype(BF16), jnp.exp2(a_last))


def _hgrn_mix(sides, v, ga, gnorm, st):
    levels, diag, qg, kd, chunk_decay = sides
    row = lax.broadcasted_iota(jnp.int32, (HGRN_CHUNK, HGRN_CHUNK), 0)
    col = lax.broadcasted_iota(jnp.int32, (HGRN_CHUNK, HGRN_CHUNK), 1)
    differ = row ^ col
    scores = jnp.where(differ == 0, diag, 0.0)
    for group, ql, kl in levels:
        x = _dot_nt(ql, kl)
        if group < HGRN_CHUNK:
            x = jnp.where(differ < group, x, 0.0)
        scores = scores + x

    o = _dot_nt(qg, st.astype(BF16)) + _dot(scores.astype(BF16), v.astype(BF16))
    st_new = st * chunk_decay + _dot(v.T.astype(BF16), kd)

    ms = jnp.mean(o * o, axis=-1, keepdims=True)
    o = o * lax.rsqrt(ms + RMS_EPS) * gnorm * (ga * jax.nn.sigmoid(ga))
    return o, st_new


def _hgrn_prompt_kernel(q_ref, f_ref, i_ref, g_ref, lbp_ref, gn_ref, o_ref, s_ref, st_scr, *, n_chunks):
    t = pl.program_id(1)

    @pl.when(t == 0)
    def _():
        st_scr[...] = jnp.zeros(st_scr.shape, F32)

    gnorm = gn_ref[...]

    def chunk_body(c, carry):
        rows = pl.ds(pl.multiple_of(c * HGRN_CHUNK, HGRN_CHUNK), HGRN_CHUNK)

        def head_lanes(h):
            return slice(h * HEAD_DIM, (h + 1) * HEAD_DIM)

        def sides_of(h):
            lanes = head_lanes(h)
            return _hgrn_sides(q_ref[rows, lanes], f_ref[rows, lanes], _lower_bound(lbp_ref[:, lanes]))

        sides = sides_of(0)
        for h in range(N_HEADS):
            lanes = head_lanes(h)
            nxt = sides_of(h + 1) if h + 1 < N_HEADS else None
            o, st_new = _hgrn_mix(sides, i_ref[rows, lanes].astype(F32), g_ref[rows, lanes].astype(F32),
                                  gnorm, st_scr[h])
            st_scr[h] = st_new
            o_ref[rows, lanes] = o.astype(o_ref.dtype)
            sides = nxt
        return carry

    lax.fori_loop(0, n_chunks, chunk_body, 0)

    @pl.when(t == pl.num_programs(1) - 1)
    def _():
        for h in range(N_HEADS):
            s_ref[h] = st_scr[h].T


def _hgrn_prompt(wide, narrow, lb_param, g_norm, batch, seq, tt):
    n_t = seq // tt

    def sec(s):
        return pl.BlockSpec((None, tt, D_MODEL), lambda b, t, s=s: (s, b * n_t + t, 0))

    return pl.pallas_call(
        functools.partial(_hgrn_prompt_kernel, n_chunks=tt // HGRN_CHUNK),
        grid=(batch, n_t),
        in_specs=[sec(WIDE_Q), sec(WIDE_F), sec(NARROW_I), sec(NARROW_G),
                  pl.BlockSpec(lb_param.shape, lambda b, t: (0, 0)),
                  pl.BlockSpec(g_norm.shape, lambda b, t: (0, 0))],
        out_specs=[pl.BlockSpec((tt, D_MODEL), lambda b, t: (b * n_t + t, 0)),
                   pl.BlockSpec((None, N_HEADS, HEAD_DIM, HEAD_DIM), lambda b, t: (b, 0, 0, 0))],
        out_shape=[jax.ShapeDtypeStruct((batch * seq, D_MODEL), BF16),
                   jax.ShapeDtypeStruct((batch, N_HEADS, HEAD_DIM, HEAD_DIM), F32)],
        scratch_shapes=[pltpu.VMEM((N_HEADS, HEAD_DIM, HEAD_DIM), F32)],
        compiler_params=_params(2),
        name="hgrn_prompt",
    )(wide, wide, narrow, narrow, lb_param, g_norm)


def _hgrn_sample_kernel(q_ref, f_ref, i_ref, g_ref, lbp_ref, gn_ref, s0_ref, o_ref, s_ref):
    sb = q_ref.shape[0]
    lb = _lower_bound(lbp_ref[...])
    f, k = _gates_from_z(f_ref[...], lb)
    qa = q_ref[...]
    q = qa * jax.nn.sigmoid(qa)
    ga = g_ref[...]
    gate = ga * jax.nn.sigmoid(ga)
    v = i_ref[...]
    gnorm = gn_ref[...]
    qf = q * f
    qk = q * k
    for h in range(N_HEADS):
        hl = slice(h * HEAD_DIM, (h + 1) * HEAD_DIM)
        f_cols = f[:, hl].T
        k_cols = k[:, hl].T
        for b in range(sb):
            s0 = s0_ref[b, h]
            v_row = v[b:b + 1, hl]
            s_ref[b, h] = s0 * f_cols[:, b:b + 1] + k_cols[:, b:b + 1] * v_row
            o = _dot(qf[b:b + 1, hl].astype(BF16), s0.astype(BF16)) \
                + jnp.sum(qk[b:b + 1, hl], axis=-1, keepdims=True) * v_row
            ms = jnp.mean(o * o, axis=-1, keepdims=True)
            o_ref[b:b + 1, hl] = o * lax.rsqrt(ms + RMS_EPS) * gnorm * gate[b:b + 1, hl]


def _hgrn_sample(wide, narrow, lb_param, g_norm, state, sb):
    n = state.shape[0]

    def sec(s):
        return pl.BlockSpec((None, sb, D_MODEL), lambda i, s=s: (s, i, 0))

    st_spec = pl.BlockSpec((sb, N_HEADS, HEAD_DIM, HEAD_DIM), lambda i: (i, 0, 0, 0))
    return pl.pallas_call(
        _hgrn_sample_kernel,
        grid=(n // sb,),
        in_specs=[sec(WIDE_Q), sec(WIDE_F), sec(NARROW_I), sec(NARROW_G),
                  pl.BlockSpec(lb_param.shape, lambda i: (0, 0)),
                  pl.BlockSpec(g_norm.shape, lambda i: (0, 0)),
                  st_spec],
        out_specs=[pl.BlockSpec((sb, D_MODEL), lambda i: (i, 0)), st_spec],
        out_shape=[jax.ShapeDtypeStruct((n, D_MODEL), F32),
                   jax.ShapeDtypeStruct(state.shape, F32)],
        compiler_params=_params(1),
        name="hgrn_sample",
    )(wide, wide, narrow, narrow, lb_param, g_norm, state)


def _alibi_slopes():
    return np.asarray([2.0 ** (-8.0 * (h + 1) / N_HEADS) for h in range(N_HEADS)], np.float32)


def _topk_rank(g, idx, n):
    axis, count = n
    rank = jnp.zeros(g.shape, F32)
    for m in range(count):
        gm = lax.slice_in_dim(g, m, m + 1, axis=axis)
        rank = rank + ((gm > g) | ((gm == g) & (idx > m))).astype(F32)
    return rank


def _moba_prompt_kernel(slopes_ref, q_ref, k_ref, v_ref, o_ref,
                        kb, vt, qtb, kmean, bias, dist, *, n_blk):
    h = pl.program_id(1)
    blk = MOBA_BLOCK
    n_pairs = n_blk // 2
    slope2 = slopes_ref[h] * LOG2E
    scale2 = HEAD_DIM ** -0.5 * LOG2E

    def blk_rows(n):
        return slice(n * blk, (n + 1) * blk)

    def prep(n, carry):
        rows = blk_rows(n)
        kk = k_ref[rows, :]
        kb[rows, :] = kk.astype(BF16)
        kmean[pl.ds(n, 1), :] = jnp.sum(kk, axis=0, keepdims=True) * (1.0 / blk)
        vt[:, rows] = v_ref[rows, :].T.astype(BF16)
        return carry

    for n in range(n_blk):
        prep(n, 0)

    key = lax.broadcasted_iota(jnp.int32, (blk, blk), 0)
    qry = lax.broadcasted_iota(jnp.int32, (blk, blk), 1)
    d0 = slope2 * (qry - key).astype(F32)
    dist[0] = d0
    dist[1] = jnp.where(key <= qry, d0, -MASKED)

    def choose(qi, carry):
        rows = blk_rows(qi)
        q_t = q_ref[rows, :].astype(F32).T
        qtb[:, rows] = (q_t * scale2).astype(BF16)
        gate = jnp.dot(kmean[...], q_t, precision=lax.Precision.HIGHEST, preferred_element_type=F32)
        nio = lax.broadcasted_iota(jnp.int32, gate.shape, 0)
        past = nio < qi
        g = jnp.where(past, gate, -jnp.inf)
        attend = (past & (_topk_rank(g, nio, (0, n_blk)) < MOBA_TOPK)) | (nio == qi)
        bias[:, rows] = jnp.where(attend, -slope2 * ((qi - nio) * blk).astype(F32), MASKED)
        return carry

    for qi in range(n_blk):
        choose(qi, 0)

    steps = [(j, i) for j in range(n_pairs) for i in range(j + 1)]

    def pair(n):
        return slice(n * 2 * blk, (n + 1) * 2 * blk)

    def raw_scores(j, i):
        return _dot(kb[pair(i), :], qtb[:, pair(j)])

    state = {}

    def flush(j, i, p, alpha):
        m, l, acc = state[j]
        acc = acc * alpha + _dot(vt[:, pair(i)], p)
        state[j] = (m, l, acc)
        if i == j:
            o_ref[pair(j), :] = (acc * (1.0 / l)).T.astype(o_ref.dtype)
            del state[j]

    s_next = raw_scores(*steps[0])
    pending = None
    for n, (j, i) in enumerate(steps):
        s = s_next
        if n + 1 < len(steps):
            s_next = raw_scores(*steps[n + 1])
        if pending is not None:
            flush(*pending)
        if i == 0:
            state[j] = (jnp.full((1, 2 * blk), M_INIT, F32), jnp.zeros((1, 2 * blk), F32),
                        jnp.zeros((HEAD_DIM, 2 * blk), F32))
        m, l, acc = state[j]
        own = 1 if i == j else 0
        ms, ls, alphas, ps = [], [], [], []
        for t in range(2 * blk // STRIP):
            lanes = slice(t * STRIP, (t + 1) * STRIP)
            in_blk = slice(t * STRIP % blk, t * STRIP % blk + STRIP)
            first = t * STRIP < blk
            xa = s[:blk, lanes] - dist[own if first else 0, :, in_blk]
            xb = s[blk:, lanes] - dist[0 if first else own, :, in_blk]
            ba = bias[2 * i:2 * i + 1, j * 2 * blk + t * STRIP:j * 2 * blk + (t + 1) * STRIP]
            bb = bias[2 * i + 1:2 * i + 2, j * 2 * blk + t * STRIP:j * 2 * blk + (t + 1) * STRIP]
            m_t = jnp.maximum(m[:, lanes], jnp.maximum(jnp.max(xa, axis=0, keepdims=True) + ba,
                                                       jnp.max(xb, axis=0, keepdims=True) + bb))
            alpha_t = jnp.exp2(m[:, lanes] - m_t)
            pa = jnp.exp2(xa - (m_t - ba))
            pb = jnp.exp2(xb - (m_t - bb))
            ls.append(alpha_t * l[:, lanes] + jnp.sum(pa, axis=0, keepdims=True)
                      + jnp.sum(pb, axis=0, keepdims=True))
            ms.append(m_t)
            alphas.append(alpha_t)
            ps.append(jnp.concatenate([pa.astype(BF16), pb.astype(BF16)], axis=0))
        state[j] = (jnp.concatenate(ms, axis=1), jnp.concatenate(ls, axis=1), acc)
        pending = (j, i, jnp.concatenate(ps, axis=1), jnp.concatenate(alphas, axis=1))
    flush(*pending)


def _moba_prompt(narrow, k_rows, v_rows, batch, seq):
    n_blk = seq // MOBA_BLOCK
    assert n_blk % 2 == 0, "key blocks are processed in pairs"
    head = pl.BlockSpec((seq, HEAD_DIM), lambda b, h: (b, h))
    return pl.pallas_call(
        functools.partial(_moba_prompt_kernel, n_blk=n_blk),
        grid=(batch, N_HEADS),
        in_specs=[pl.BlockSpec(memory_space=pltpu.SMEM),
                  pl.BlockSpec((None, seq, HEAD_DIM), lambda b, h: (NARROW_MQ, b, h)),
                  head, head],
        out_specs=head,
        out_shape=jax.ShapeDtypeStruct((batch * seq, D_MODEL), BF16),
        scratch_shapes=[pltpu.VMEM((seq, HEAD_DIM), BF16),
                        pltpu.VMEM((HEAD_DIM, seq), BF16),
                        pltpu.VMEM((HEAD_DIM, seq), BF16),
                        pltpu.VMEM((n_blk, HEAD_DIM), F32),
                        pltpu.VMEM((n_blk, seq), F32),
                        pltpu.VMEM((2, MOBA_BLOCK, MOBA_BLOCK), F32)],
        compiler_params=_params(2),
        name="moba_prompt",
    )(jnp.asarray(_alibi_slopes()), narrow, k_rows, v_rows)


def _page_heads_on_lanes(page_ref):
    return jnp.concatenate([page_ref[pl.ds(h, PAGE_SIZE, stride=N_HEADS), :] for h in range(N_HEADS)], axis=1)


def _moba_sample_kernel(pt_ref, slopes_ref, q_ref, kn_ref, vn_ref, *rest, n_pages, group):
    del pt_ref
    k_refs, v_refs = rest[:group], rest[group:2 * group]
    o_ref, sc, ksum, p_scr, acc, linv = rest[2 * group:]
    j = pl.program_id(1)
    k_steps = n_pages // group
    past_len = n_pages * PAGE_SIZE
    n_blk = past_len // MOBA_BLOCK
    pages_per_blk = MOBA_BLOCK // PAGE_SIZE
    blks_per_step = group // pages_per_blk
    scale = HEAD_DIM ** -0.5
    q = q_ref[...].astype(F32)
    head_of_lane = lax.shift_right_logical(
        lax.broadcasted_iota(jnp.int32, (N_HEADS, D_MODEL), 1), HEAD_DIM.bit_length() - 1)
    head_of_row = lax.broadcasted_iota(jnp.int32, (N_HEADS, D_MODEL), 0)
    own_lanes = head_of_lane == head_of_row
    qr = jnp.where(own_lanes, jnp.broadcast_to(q, (N_HEADS, D_MODEL)), 0.0)

    @pl.when(j < k_steps)
    def _():
        qrb = qr.astype(BF16)
        sums = []
        for i in range(group):
            kp = _page_heads_on_lanes(k_refs[i])
            page = j * group + i
            sc[:, pl.ds(pl.multiple_of(page * PAGE_SIZE, PAGE_SIZE), PAGE_SIZE)] = _dot_nt(qrb, kp.astype(BF16))
            sums.append(jnp.sum(kp, axis=0, keepdims=True))
        for b in range(blks_per_step):
            tot = sums[b * pages_per_blk]
            for r in range(1, pages_per_blk):
                tot = tot + sums[b * pages_per_blk + r]
            ksum[pl.ds(j * blks_per_step + b, 1), :] = tot

    @pl.when(j == k_steps - 1)
    def _():
        kmean = ksum[...] * (1.0 / MOBA_BLOCK)
        gate = _dot_nt(qr, kmean, precision=lax.Precision.HIGHEST)
        nio = lax.broadcasted_iota(jnp.int32, gate.shape, 1)
        sel = _topk_rank(gate, nio, (1, n_blk)) < MOBA_TOPK
        kpos = lax.broadcasted_iota(jnp.int32, (N_HEADS, past_len), 1)
        blk_of_key = lax.shift_right_logical(kpos, MOBA_BLOCK.bit_length() - 1)
        chosen = jnp.zeros((N_HEADS, past_len), jnp.bool_)
        for n in range(n_blk):
            chosen = chosen | ((blk_of_key == n) & sel[:, n:n + 1])
        slope = slopes_ref[...]
        s_all = sc[...] * scale - slope * (past_len - kpos).astype(F32)
        s_all = jnp.where(chosen, s_all, MASKED)
        s_self = jnp.sum(qr * kn_ref[...], axis=-1, keepdims=True) * scale
        m = jnp.maximum(jnp.max(s_all, axis=-1, keepdims=True), s_self)
        p = jnp.exp(s_all - m)
        p_self = jnp.exp(s_self - m)
        p_scr[...] = p
        linv[...] = jnp.broadcast_to(1.0 / (jnp.sum(p, axis=-1, keepdims=True) + p_self), linv.shape)
        acc[...] = p_self * jnp.broadcast_to(vn_ref[...], (N_HEADS, D_MODEL))

    @pl.when(j >= k_steps)
    def _():
        part = None
        for i in range(group):
            page = (j - k_steps) * group + i
            pj = p_scr[:, pl.ds(pl.multiple_of(page * PAGE_SIZE, PAGE_SIZE), PAGE_SIZE)]
            d = _dot(pj.astype(BF16), _page_heads_on_lanes(v_refs[i]).astype(BF16))
            part = d if part is None else part + d
        acc[...] += part

    @pl.when(j == 2 * k_steps - 1)
    def _():
        a = acc[...] * linv[:, 0:1]
        for h in range(N_HEADS):
            hl = slice(h * HEAD_DIM, (h + 1) * HEAD_DIM)
            o_ref[:, hl] = a[h:h + 1, hl]


def _moba_sample(narrow3, k_new, v_new, cache_k, cache_v, page_table, group):
    n_seq, n_pages = page_table.shape
    past_len = n_pages * PAGE_SIZE
    assert past_len % MOBA_BLOCK == 0, "the decode token must start a MoBA block"
    assert n_pages % group == 0 and group % (MOBA_BLOCK // PAGE_SIZE) == 0
    n_blk = past_len // MOBA_BLOCK
    k_steps = n_pages // group
    page_rows = PAGE_SIZE * N_HEADS
    new_row = pl.BlockSpec((None, 1, D_MODEL), lambda i, j, pt: (i, 0, 0))

    def k_page(g):
        return pl.BlockSpec((page_rows, HEAD_DIM), lambda i, j, pt: (
            pt[i * n_pages + jnp.minimum(j, k_steps - 1) * group + g], 0))

    def v_page(g):
        def index(i, j, pt):
            in_k = j < k_steps
            seq_i = jnp.maximum(i - jnp.where(in_k, 1, 0), 0)
            grp = jnp.where(in_k, k_steps - 1, j - k_steps)
            return pt[seq_i * n_pages + grp * group + g], 0
        return pl.BlockSpec((page_rows, HEAD_DIM), index)

    grid_spec = pltpu.PrefetchScalarGridSpec(
        num_scalar_prefetch=1,
        grid=(n_seq, 2 * k_steps),
        in_specs=[pl.BlockSpec((N_HEADS, 1), lambda i, j, pt: (0, 0)),
                  pl.BlockSpec((None, None, 1, D_MODEL), lambda i, j, pt: (NARROW_MQ, i, 0, 0)), new_row, new_row]
                 + [k_page(g) for g in range(group)] + [v_page(g) for g in range(group)],
        out_specs=new_row,
        scratch_shapes=[pltpu.VMEM((N_HEADS, past_len), F32),
                        pltpu.VMEM((n_blk, D_MODEL), F32),
                        pltpu.VMEM((N_HEADS, past_len), F32),
                        pltpu.VMEM((N_HEADS, D_MODEL), F32),
                        pltpu.VMEM((N_HEADS, HEAD_DIM), F32)])
    return pl.pallas_call(
        functools.partial(_moba_sample_kernel, n_pages=n_pages, group=group),
        grid_spec=grid_spec,
        out_shape=jax.ShapeDtypeStruct((n_seq, 1, D_MODEL), F32),
        compiler_params=_params(2),
        name="moba_sample",
    )(page_table.reshape(-1), jnp.asarray(_alibi_slopes()).reshape(N_HEADS, 1),
      narrow3, k_new, v_new, *([cache_k] * group), *([cache_v] * group))


def _layer_norm(x, g, b):
    mu = jnp.mean(x, axis=-1, keepdims=True)
    xc = x - mu
    var = jnp.mean(xc * xc, axis=-1, keepdims=True)
    return xc * lax.rsqrt(var + LN_EPS) * g + b


def _mix_kernel(oa_ref, ob_ref, ga_ref, gb_ref, x_ref, wa_ref, wb_ref, wo_ref, g_ref, b_ref, h_ref):
    ya = _dot(oa_ref[...].astype(BF16), wa_ref[...])
    yb = _dot(ob_ref[...].astype(BF16), wb_ref[...])
    mixed = jax.nn.sigmoid(ga_ref[...].astype(F32)) * ya + jax.nn.sigmoid(gb_ref[...].astype(F32)) * yb
    mix = _dot(mixed.astype(BF16), wo_ref[...])
    h_ref[...] = _layer_norm(DEEPNORM_ALPHA * x_ref[...] + mix, g_ref[...], b_ref[...])


def _mix(o_a, o_b, narrow, x2d, wa, wb, wo, ln_g, ln_b, tm):
    m = x2d.shape[0]
    rows = pl.BlockSpec((tm, D_MODEL), lambda i: (i, 0))
    sec = lambda s: pl.BlockSpec((None, tm, D_MODEL), lambda i, s=s: (s, i, 0))
    full = lambda a: pl.BlockSpec(a.shape, lambda i: (0, 0))
    return pl.pallas_call(
        _mix_kernel,
        grid=(m // tm,),
        in_specs=[rows, rows, sec(NARROW_GA), sec(NARROW_GB), rows, full(wa), full(wb), full(wo), full(ln_g), full(ln_b)],
        out_specs=rows,
        out_shape=jax.ShapeDtypeStruct((m, D_MODEL), F32),
        compiler_params=_params(1),
        name="mix",
    )(o_a, o_b, narrow, narrow, x2d, wa, wb, wo, ln_g, ln_b)


def _mlp_kernel(h_ref, wu_ref, wd_ref, g_ref, b_ref, y_ref, acc, hb):
    j = pl.program_id(1)

    @pl.when(j == 0)
    def _():
        hb[...] = h_ref[...].astype(BF16)
        acc[...] = jnp.zeros(acc.shape, F32)

    u = jnp.maximum(_dot(hb[...], wu_ref[...]), 0.0)
    acc[...] += _dot((u * u).astype(BF16), wd_ref[...])

    @pl.when(j == pl.num_programs(1) - 1)
    def _():
        y_ref[...] = _layer_norm(DEEPNORM_ALPHA * h_ref[...] + acc[...], g_ref[...], b_ref[...])


def _mlp(h, wu, wd, ln_g, ln_b, tm, tf):
    m = h.shape[0]
    d_ff = wu.shape[1]
    rows = pl.BlockSpec((tm, D_MODEL), lambda i, j: (i, 0))
    full = lambda a: pl.BlockSpec(a.shape, lambda i, j: (0, 0))
    return pl.pallas_call(
        _mlp_kernel,
        grid=(m // tm, d_ff // tf),
        in_specs=[rows,
                  pl.BlockSpec((D_MODEL, tf), lambda i, j: (0, j)),
                  pl.BlockSpec((tf, D_MODEL), lambda i, j: (j, 0)),
                  full(ln_g), full(ln_b)],
        out_specs=rows,
        out_shape=jax.ShapeDtypeStruct((m, D_MODEL), F32),
        scratch_shapes=[pltpu.VMEM((tm, D_MODEL), F32), pltpu.VMEM((tm, D_MODEL), BF16)],
        compiler_params=_params(2),
        name="mlp",
    )(h, wu, wd, ln_g, ln_b)


def kernel(x_prompt, x_sample, cache_k, cache_v, state_hgrn, page_table, w_in, lb_param, g_norm_a,
           w_proj_a, w_proj_b, w_out, ln1_g, ln1_b, w_up, w_down, ln2_g, ln2_b):
    assert w_in.shape[0] == DEPTH
    batch, seq, _ = x_prompt.shape
    n_seq, dec_seq, _ = x_sample.shape
    assert dec_seq == 1
    n_pool = cache_k.shape[1]

    w_in_b = w_in[0].astype(BF16)
    wa, wb, wo = w_proj_a[0].astype(BF16), w_proj_b[0].astype(BF16), w_out[0].astype(BF16)
    wu, wd = w_up[0].astype(BF16), w_down[0].astype(BF16)
    gn = g_norm_a[0].reshape(1, HEAD_DIM)
    g1, b1 = ln1_g[0].reshape(1, D_MODEL), ln1_b[0].reshape(1, D_MODEL)
    g2, b2 = ln2_g[0].reshape(1, D_MODEL), ln2_b[0].reshape(1, D_MODEL)

    xp = x_prompt.reshape(batch * seq, D_MODEL)
    wide_p, narrow_p, k_p, v_p = _inproj(xp, w_in_b, tm=1024)
    oa_p, st_p = _hgrn_prompt(wide_p, narrow_p, lb_param, gn, batch, seq, tt=512)
    ob_p = _moba_prompt(narrow_p, k_p, v_p, batch, seq)
    h_p = _mix(oa_p, ob_p, narrow_p, xp, wa, wb, wo, g1, b1, tm=512)
    y_p = _mlp(h_p, wu, wd, g2, b2, tm=1024, tf=1024)

    xs = x_sample.reshape(n_seq, D_MODEL)
    wide_s, narrow_s, k_s, v_s = _inproj(xs, w_in_b, tm=n_seq)
    narrow_s = narrow_s.astype(F32)
    oa_s, st_s = _hgrn_sample(wide_s, narrow_s, lb_param, gn, state_hgrn[0], sb=8)
    ob_s = _moba_sample(narrow_s.reshape(N_NARROW, n_seq, 1, D_MODEL),
                        k_s.reshape(n_seq, 1, D_MODEL), v_s.reshape(n_seq, 1, D_MODEL),
                        cache_k[0].reshape(n_pool * PAGE_SIZE * N_HEADS, HEAD_DIM),
                        cache_v[0].reshape(n_pool * PAGE_SIZE * N_HEADS, HEAD_DIM), page_table, group=16)
    h_s = _mix(oa_s, ob_s.reshape(n_seq, D_MODEL), narrow_s, xs, wa, wb, wo, g1, b1, tm=n_seq)
    y_s = _mlp(h_s, wu, wd, g2, b2, tm=n_seq, tf=1024)

    kv_p = (DEPTH, batch, seq, N_HEADS, HEAD_DIM)
    kv_s = (DEPTH, n_seq, dec_seq, N_HEADS, HEAD_DIM)
    return (y_p.reshape(batch, seq, D_MODEL), y_s.reshape(n_seq, dec_seq, D_MODEL),
            k_p.reshape(kv_p), v_p.reshape(kv_p), st_p[None],
            k_s.reshape(kv_s), v_s.reshape(kv_s), st_s[None])
```

```python
import functools

import numpy as np
import jax
import jax.numpy as jnp
from jax import lax
from jax.experimental import pallas as pl
from jax.experimental.pallas import tpu as pltpu

F32 = jnp.float32
BF16 = jnp.bfloat16

D_MODEL = 1024
N_HEADS = 8
HEAD_DIM = 128
N_COLS = 9
COL_MK, COL_MV = 5, 6
N_WIDE = 2
WIDE_Q, WIDE_F = range(N_WIDE)
N_NARROW = 5
NARROW_I, NARROW_G, NARROW_MQ, NARROW_GA, NARROW_GB = range(N_NARROW)
DEPTH = 1
LN_EPS = 1e-5
RMS_EPS = 1e-6
DEEPNORM_ALPHA = (2.0 * DEPTH) ** 0.25
MOBA_BLOCK = 256
MOBA_TOPK = 3
PAGE_SIZE = 128
MASKED = -2e30
M_INIT = -1e30
LOG2E = 1.4426950408889634
SUBLANES = 8
STRIP = 128
HGRN_CHUNK = 128
VMEM_LIMIT = 48 * 1024 * 1024

_NT = (((1,), (1,)), ((), ()))


def _dot(a, b):
    return jnp.dot(a, b, preferred_element_type=F32)


def _dot_nt(a, b, precision=None):
    return lax.dot_general(a, b, _NT, precision=precision, preferred_element_type=F32)


def _params(n_grid):
    return pltpu.CompilerParams(dimension_semantics=("arbitrary",) * n_grid,
                                vmem_limit_bytes=VMEM_LIMIT)


def _inproj_kernel(x_ref, w_ref, wide_ref, narrow_ref, k_ref, v_ref, xb):
    j = pl.program_id(1)

    @pl.when(j == 0)
    def _():
        xb[...] = x_ref[...].astype(BF16)

    y = _dot(xb[...], w_ref[...])

    @pl.when(j < N_WIDE)
    def _():
        wide_ref[...] = y

    @pl.when(j == COL_MK)
    def _():
        k_ref[...] = y

    @pl.when(j == COL_MV)
    def _():
        v_ref[...] = y

    @pl.when((j >= N_WIDE) & (j != COL_MK) & (j != COL_MV))
    def _():
        narrow_ref[...] = y.astype(narrow_ref.dtype)


def _inproj(x2d, w_bf16, tm):
    m = x2d.shape[0]

    def wide(i, j):
        return jnp.minimum(j, N_WIDE - 1), i, 0

    def narrow(i, j):
        sec = jnp.where(j < COL_MK, j - N_WIDE, jnp.where(j > COL_MV, j - N_WIDE - 2, COL_MK - 1 - N_WIDE))
        return jnp.maximum(sec, 0), i, 0

    rows = pl.BlockSpec((tm, D_MODEL), lambda i, j: (i, 0))
    return pl.pallas_call(
        _inproj_kernel,
        grid=(m // tm, N_COLS),
        in_specs=[rows, pl.BlockSpec((D_MODEL, D_MODEL), lambda i, j: (0, j))],
        out_specs=[pl.BlockSpec((None, tm, D_MODEL), wide), pl.BlockSpec((None, tm, D_MODEL), narrow), rows, rows],
        out_shape=[jax.ShapeDtypeStruct((N_WIDE, m, D_MODEL), F32),
                   jax.ShapeDtypeStruct((N_NARROW, m, D_MODEL), BF16),
                   jax.ShapeDtypeStruct((m, D_MODEL), F32),
                   jax.ShapeDtypeStruct((m, D_MODEL), F32)],
        scratch_shapes=[pltpu.VMEM((tm, D_MODEL), BF16)],
        compiler_params=_params(2),
        name="inproj",
    )(x2d, w_bf16)


def _lower_bound(p):
    m = p[0:1]
    for r in range(1, p.shape[0]):
        m = jnp.maximum(m, p[r:r + 1])
    e = [jnp.exp(p[r:r + 1] - m) for r in range(p.shape[0])]
    tot = e[0]
    for r in range(1, p.shape[0]):
        tot = tot + e[r]
    return e[0] / tot


def _gates_from_z(z, lb):
    e = jnp.exp(-jnp.abs(z))
    r = 1.0 / (1.0 + e)
    er = e * r
    pos = z >= 0
    oml = 1.0 - lb
    f = lb + oml * jnp.where(pos, r, er)
    k = oml * jnp.where(pos, er, r)
    return f, k


def _bc8(row):
    return jnp.broadcast_to(row, (SUBLANES, row.shape[-1]))


def _tile_levels(x, q, k, row_masks):
    ninf = -jnp.inf
    sides = []
    odd, upper_pair, upper_half = row_masks
    up = odd
    tot = pltpu.roll(x, 1, 0)
    sides.append((q * jnp.exp2(jnp.where(up, x, ninf)), jnp.where(up, 0.0, k)))
    sc = x + jnp.where(up, tot, 0.0)
    up = upper_pair
    tot = jnp.where(upper_half, _bc8(sc[5:6]), _bc8(sc[1:2]))
    sides.append((q * jnp.exp2(jnp.where(up, sc, ninf)), k * jnp.exp2(jnp.where(up, ninf, tot - sc))))
    sc = sc + jnp.where(up, tot, 0.0)
    up = upper_half
    tot = _bc8(sc[3:4])
    sides.append((q * jnp.exp2(jnp.where(up, sc, ninf)), k * jnp.exp2(jnp.where(up, ninf, tot - sc))))
    sc = sc + jnp.where(up, tot, 0.0)
    return sc, sides


def _hgrn_sides(qa, z, lb):
    n_tiles = HGRN_CHUNK // SUBLANES
    f, k = _gates_from_z(z, lb)
    logf = jnp.log2(f)
    q = qa * jax.nn.sigmoid(qa)

    def tiles(a):
        return [a[SUBLANES * j:SUBLANES * (j + 1)] for j in range(n_tiles)]

    qt, kt = tiles(q), tiles(k)
    r8 = lax.broadcasted_iota(jnp.int32, (SUBLANES, HEAD_DIM), 0)
    row_masks = ((r8 & 1) == 1, (r8 & 2) == 2, r8 >= 4)
    in_tile = [_tile_levels(x, qt[j], kt[j], row_masks) for j, x in enumerate(tiles(logf))]
    levels = [(2 << b, jnp.concatenate([in_tile[j][1][b][0] for j in range(n_tiles)], axis=0),
               jnp.concatenate([in_tile[j][1][b][1] for j in range(n_tiles)], axis=0)) for b in range(3)]

    sc = [t[0] for t in in_tile]
    zero = jnp.zeros((SUBLANES, HEAD_DIM), F32)
    nb = 1
    while nb < n_tiles:
        ql = [zero] * n_tiles
        kl = [zero] * n_tiles
        new = list(sc)
        for m in range(n_tiles // (2 * nb)):
            lo = range(2 * m * nb, (2 * m + 1) * nb)
            up = range((2 * m + 1) * nb, (2 * m + 2) * nb)
            tot = _bc8(sc[lo[-1]][SUBLANES - 1:SUBLANES])
            for j in lo:
                kl[j] = kt[j] * jnp.exp2(tot - sc[j])
            for j in up:
                ql[j] = qt[j] * jnp.exp2(sc[j])
                new[j] = sc[j] + tot
        levels.append((2 * nb * SUBLANES, jnp.concatenate(ql, axis=0), jnp.concatenate(kl, axis=0)))
        sc = new
        nb *= 2
    a_t = sc
    a_last = a_t[n_tiles - 1][SUBLANES - 1:SUBLANES]
    qg = jnp.concatenate([qt[j] * jnp.exp2(a_t[j]) for j in range(n_tiles)], axis=0)
    kd = jnp.concatenate([kt[j] * jnp.exp2(_bc8(a_last) - a_t[j]) for j in range(n_tiles)], axis=0)
    diag = jnp.sum(q * k, axis=-1, keepdims=True)
    return ([(group, ql.astype(BF16), kl.astype(BF16)) for group, ql, kl in levels],
            diag, qg.astype(BF16), kd.astype(BF16), jnp.exp2(a_last))


def _hgrn_mix(sides, v, ga, gnorm, st):
    levels, diag, qg, kd, chunk_decay = sides
    row = lax.broadcasted_iota(jnp.int32, (HGRN_CHUNK, HGRN_CHUNK), 0)
    col = lax.broadcasted_iota(jnp.int32, (HGRN_CHUNK, HGRN_CHUNK), 1)
    differ = row ^ col
    scores = jnp.where(differ == 0, diag, 0.0)
    for group, ql, kl in levels:
        x = _dot_nt(ql, kl)
        if group < HGRN_CHUNK:
            x = jnp.where(differ < group, x, 0.0)
        scores = scores + x

    o = _dot_nt(qg, st.astype(BF16)) + _dot(scores.astype(BF16), v.astype(BF16))
    st_new = st * chunk_decay + _dot(v.T.astype(BF16), kd)

    ms = jnp.mean(o * o, axis=-1, keepdims=True)
    o = o * lax.rsqrt(ms + RMS_EPS) * gnorm * (ga * jax.nn.sigmoid(ga))
    return o, st_new


def _hgrn_prompt_kernel(q_ref, f_ref, i_ref, g_ref, lbp_ref, gn_ref, o_ref, s_ref, st_scr, *, n_chunks):
    t = pl.program_id(1)

    @pl.when(t == 0)
    def _():
        st_scr[...] = jnp.zeros(st_scr.shape, F32)

    gnorm = gn_ref[...]

    def chunk_body(c, carry):
        rows = pl.ds(pl.multiple_of(c * HGRN_CHUNK, HGRN_CHUNK), HGRN_CHUNK)

        def head_lanes(h):
            return slice(h * HEAD_DIM, (h + 1) * HEAD_DIM)

        def sides_of(h):
            lanes = head_lanes(h)
            return _hgrn_sides(q_ref[rows, lanes], f_ref[rows, lanes], _lower_bound(lbp_ref[:, lanes]))

        sides = sides_of(0)
        for h in range(N_HEADS):
            lanes = head_lanes(h)
            nxt = sides_of(h + 1) if h + 1 < N_HEADS else None
            o, st_new = _hgrn_mix(sides, i_ref[rows, lanes].astype(F32), g_ref[rows, lanes].astype(F32),
                                  gnorm, st_scr[h])
            st_scr[h] = st_new
            o_ref[rows, lanes] = o.astype(o_ref.dtype)
            sides = nxt
        return carry

    lax.fori_loop(0, n_chunks, chunk_body, 0)

    @pl.when(t == pl.num_programs(1) - 1)
    def _():
        for h in range(N_HEADS):
            s_ref[h] = st_scr[h].T


def _hgrn_prompt(wide, narrow, lb_param, g_norm, batch, seq, tt):
    n_t = seq // tt

    def sec(s):
        return pl.BlockSpec((None, tt, D_MODEL), lambda b, t, s=s: (s, b * n_t + t, 0))

    return pl.pallas_call(
        functools.partial(_hgrn_prompt_kernel, n_chunks=tt // HGRN_CHUNK),
        grid=(batch, n_t),
        in_specs=[sec(WIDE_Q), sec(WIDE_F), sec(NARROW_I), sec(NARROW_G),
                  pl.BlockSpec(lb_param.shape, lambda b, t: (0, 0)),
                  pl.BlockSpec(g_norm.shape, lambda b, t: (0, 0))],
        out_specs=[pl.BlockSpec((tt, D_MODEL), lambda b, t: (b * n_t + t, 0)),
                   pl.BlockSpec((None, N_HEADS, HEAD_DIM, HEAD_DIM), lambda b, t: (b, 0, 0, 0))],
        out_shape=[jax.ShapeDtypeStruct((batch * seq, D_MODEL), BF16),
                   jax.ShapeDtypeStruct((batch, N_HEADS, HEAD_DIM, HEAD_DIM), F32)],
        scratch_shapes=[pltpu.VMEM((N_HEADS, HEAD_DIM, HEAD_DIM), F32)],
        compiler_params=_params(2),
        name="hgrn_prompt",
    )(wide, wide, narrow, narrow, lb_param, g_norm)


def _hgrn_sample_kernel(q_ref, f_ref, i_ref, g_ref, lbp_ref, gn_ref, s0_ref, o_ref, s_ref):
    sb = q_ref.shape[0]
    lb = _lower_bound(lbp_ref[...])
    f, k = _gates_from_z(f_ref[...], lb)
    qa = q_ref[...]
    q = qa * jax.nn.sigmoid(qa)
    ga = g_ref[...]
    gate = ga * jax.nn.sigmoid(ga)
    v = i_ref[...]
    gnorm = gn_ref[...]
    qf = q * f
    qk = q * k
    for h in range(N_HEADS):
        hl = slice(h * HEAD_DIM, (h + 1) * HEAD_DIM)
        f_cols = f[:, hl].T
        k_cols = k[:, hl].T
        for b in range(sb):
            s0 = s0_ref[b, h]
            v_row = v[b:b + 1, hl]
            s_ref[b, h] = s0 * f_cols[:, b:b + 1] + k_cols[:, b:b + 1] * v_row
            o = _dot(qf[b:b + 1, hl].astype(BF16), s0.astype(BF16)) \
                + jnp.sum(qk[b:b + 1, hl], axis=-1, keepdims=True) * v_row
            ms = jnp.mean(o * o, axis=-1, keepdims=True)
            o_ref[b:b + 1, hl] = o * lax.rsqrt(ms + RMS_EPS) * gnorm * gate[b:b + 1, hl]


def _hgrn_sample(wide, narrow, lb_param, g_norm, state, sb):
    n = state.shape[0]

    def sec(s):
        return pl.BlockSpec((None, sb, D_MODEL), lambda i, s=s: (s, i, 0))

    st_spec = pl.BlockSpec((sb, N_HEADS, HEAD_DIM, HEAD_DIM), lambda i: (i, 0, 0, 0))
    return pl.pallas_call(
        _hgrn_sample_kernel,
        grid=(n // sb,),
        in_specs=[sec(WIDE_Q), sec(WIDE_F), sec(NARROW_I), sec(NARROW_G),
                  pl.BlockSpec(lb_param.shape, lambda i: (0, 0)),
                  pl.BlockSpec(g_norm.shape, lambda i: (0, 0)),
                  st_spec],
        out_specs=[pl.BlockSpec((sb, D_MODEL), lambda i: (i, 0)), st_spec],
        out_shape=[jax.ShapeDtypeStruct((n, D_MODEL), F32),
                   jax.ShapeDtypeStruct(state.shape, F32)],
        compiler_params=_params(1),
        name="hgrn_sample",
    )(wide, wide, narrow, narrow, lb_param, g_norm, state)


def _alibi_slopes():
    return np.asarray([2.0 ** (-8.0 * (h + 1) / N_HEADS) for h in range(N_HEADS)], np.float32)


def _topk_rank(g, idx, n):
    axis, count = n
    rank = jnp.zeros(g.shape, F32)
    for m in range(count):
        gm = lax.slice_in_dim(g, m, m + 1, axis=axis)
        rank = rank + ((gm > g) | ((gm == g) & (idx > m))).astype(F32)
    return rank


def _moba_prompt_kernel(slopes_ref, q_ref, k_ref, v_ref, o_ref,
                        kb, vt, qtb, kmean, bias, dist, *, n_blk):
    h = pl.program_id(1)
    blk = MOBA_BLOCK
    n_pairs = n_blk // 2
    slope2 = slopes_ref[h] * LOG2E
    scale2 = HEAD_DIM ** -0.5 * LOG2E

    def blk_rows(n):
        return slice(n * blk, (n + 1) * blk)

    def prep(n, carry):
        rows = blk_rows(n)
        kk = k_ref[rows, :]
        kb[rows, :] = kk.astype(BF16)
        kmean[pl.ds(n, 1), :] = jnp.sum(kk, axis=0, keepdims=True) * (1.0 / blk)
        vt[:, rows] = v_ref[rows, :].T.astype(BF16)
        return carry

    for n in range(n_blk):
        prep(n, 0)

    key = lax.broadcasted_iota(jnp.int32, (blk, blk), 0)
    qry = lax.broadcasted_iota(jnp.int32, (blk, blk), 1)
    d0 = slope2 * (qry - key).astype(F32)
    dist[0] = d0
    dist[1] = jnp.where(key <= qry, d0, -MASKED)

    def choose(qi, carry):
        rows = blk_rows(qi)
        q_t = q_ref[rows, :].astype(F32).T
        qtb[:, rows] = (q_t * scale2).astype(BF16)
        gate = jnp.dot(kmean[...], q_t, precision=lax.Precision.HIGHEST, preferred_element_type=F32)
        nio = lax.broadcasted_iota(jnp.int32, gate.shape, 0)
        past = nio < qi
        g = jnp.where(past, gate, -jnp.inf)
        attend = (past & (_topk_rank(g, nio, (0, n_blk)) < MOBA_TOPK)) | (nio == qi)
        bias[:, rows] = jnp.where(attend, -slope2 * ((qi - nio) * blk).astype(F32), MASKED)
        return carry

    for qi in range(n_blk):
        choose(qi, 0)

    steps = [(j, i) for j in range(n_pairs) for i in range(j + 1)]

    def pair(n):
        return slice(n * 2 * blk, (n + 1) * 2 * blk)

    def raw_scores(j, i):
        return _dot(kb[pair(i), :], qtb[:, pair(j)])

    state = {}

    def flush(j, i, p, alpha):
        m, l, acc = state[j]
        acc = acc * alpha + _dot(vt[:, pair(i)], p)
        state[j] = (m, l, acc)
        if i == j:
            o_ref[pair(j), :] = (acc * (1.0 / l)).T.astype(o_ref.dtype)
            del state[j]

    s_next = raw_scores(*steps[0])
    pending = None
    for n, (j, i) in enumerate(steps):
        s = s_next
        if n + 1 < len(steps):
            s_next = raw_scores(*steps[n + 1])
        if pending is not None:
            flush(*pending)
        if i == 0:
            state[j] = (jnp.full((1, 2 * blk), M_INIT, F32), jnp.zeros((1, 2 * blk), F32),
                        jnp.zeros((HEAD_DIM, 2 * blk), F32))
        m, l, acc = state[j]
        own = 1 if i == j else 0
        ms, ls, alphas, ps = [], [], [], []
        for t in range(2 * blk // STRIP):
            lanes = slice(t * STRIP, (t + 1) * STRIP)
            in_blk = slice(t * STRIP % blk, t * STRIP % blk + STRIP)
            first = t * STRIP < blk
            xa = s[:blk, lanes] - dist[own if first else 0, :, in_blk]
            xb = s[blk:, lanes] - dist[0 if first else own, :, in_blk]
            ba = bias[2 * i:2 * i + 1, j * 2 * blk + t * STRIP:j * 2 * blk + (t + 1) * STRIP]
            bb = bias[2 * i + 1:2 * i + 2, j * 2 * blk + t * STRIP:j * 2 * blk + (t + 1) * STRIP]
            m_t = jnp.maximum(m[:, lanes], jnp.maximum(jnp.max(xa, axis=0, keepdims=True) + ba,
                                                       jnp.max(xb, axis=0, keepdims=True) + bb))
            alpha_t = jnp.exp2(m[:, lanes] - m_t)
            pa = jnp.exp2(xa - (m_t - ba))
            pb = jnp.exp2(xb - (m_t - bb))
            ls.append(alpha_t * l[:, lanes] + jnp.sum(pa, axis=0, keepdims=True)
                      + jnp.sum(pb, axis=0, keepdims=True))
            ms.append(m_t)
            alphas.append(alpha_t)
            ps.append(jnp.concatenate([pa.astype(BF16), pb.astype(BF16)], axis=0))
        state[j] = (jnp.concatenate(ms, axis=1), jnp.concatenate(ls, axis=1), acc)
        pending = (j, i, jnp.concatenate(ps, axis=1), jnp.concatenate(alphas, axis=1))
    flush(*pending)


def _moba_prompt(narrow, k_rows, v_rows, batch, seq):
    n_blk = seq // MOBA_BLOCK
    assert n_blk % 2 == 0, "key blocks are processed in pairs"
    head = pl.BlockSpec((seq, HEAD_DIM), lambda b, h: (b, h))
    return pl.pallas_call(
        functools.partial(_moba_prompt_kernel, n_blk=n_blk),
        grid=(batch, N_HEADS),
        in_specs=[pl.BlockSpec(memory_space=pltpu.SMEM),
                  pl.BlockSpec((None, seq, HEAD_DIM), lambda b, h: (NARROW_MQ, b, h)),
                  head, head],
        out_specs=head,
        out_shape=jax.ShapeDtypeStruct((batch * seq, D_MODEL), BF16),
        scratch_shapes=[pltpu.VMEM((seq, HEAD_DIM), BF16),
                        pltpu.VMEM((HEAD_DIM, seq), BF16),
                        pltpu.VMEM((HEAD_DIM, seq), BF16),
                        pltpu.VMEM((n_blk, HEAD_DIM), F32),
                        pltpu.VMEM((n_blk, seq), F32),
                        pltpu.VMEM((2, MOBA_BLOCK, MOBA_BLOCK), F32)],
        compiler_params=_params(2),
        name="moba_prompt",
    )(jnp.asarray(_alibi_slopes()), narrow, k_rows, v_rows)


def _page_heads_on_lanes(page_ref):
    return jnp.concatenate([page_ref[pl.ds(h, PAGE_SIZE, stride=N_HEADS), :] for h in range(N_HEADS)], axis=1)


def _moba_sample_kernel(pt_ref, slopes_ref, q_ref, kn_ref, vn_ref, *rest, n_pages):
    del pt_ref
    k_refs, v_refs = rest[:n_pages], rest[n_pages:2 * n_pages]
    o_ref, sc, ksum, p_scr, acc0, linv = rest[2 * n_pages:]
    i = pl.program_id(0)
    n_seq = pl.num_programs(0) - 1
    past_len = n_pages * PAGE_SIZE
    n_blk = past_len // MOBA_BLOCK
    pages_per_blk = MOBA_BLOCK // PAGE_SIZE
    scale = HEAD_DIM ** -0.5

    @pl.when(i > 0)
    def _():
        slot = lax.rem(i - 1, 2)
        a = acc0[slot]
        for g in range(n_pages):
            pj = p_scr[slot, :, g * PAGE_SIZE:(g + 1) * PAGE_SIZE]
            a = a + _dot(pj.astype(BF16), _page_heads_on_lanes(v_refs[g]).astype(BF16))
        a = a * linv[slot][:, 0:1]
        for h in range(N_HEADS):
            hl = slice(h * HEAD_DIM, (h + 1) * HEAD_DIM)
            o_ref[:, hl] = a[h:h + 1, hl]

    @pl.when(i < n_seq)
    def _():
        slot = lax.rem(i, 2)
        q = q_ref[...].astype(F32)
        head_of_lane = lax.shift_right_logical(
            lax.broadcasted_iota(jnp.int32, (N_HEADS, D_MODEL), 1), HEAD_DIM.bit_length() - 1)
        head_of_row = lax.broadcasted_iota(jnp.int32, (N_HEADS, D_MODEL), 0)
        qr = jnp.where(head_of_lane == head_of_row, jnp.broadcast_to(q, (N_HEADS, D_MODEL)), 0.0)
        qrb = qr.astype(BF16)
        sums = []
        for g in range(n_pages):
            kp = _page_heads_on_lanes(k_refs[g])
            sc[:, g * PAGE_SIZE:(g + 1) * PAGE_SIZE] = _dot_nt(qrb, kp.astype(BF16))
            sums.append(jnp.sum(kp, axis=0, keepdims=True))
        for b in range(n_blk):
            tot = sums[b * pages_per_blk]
            for r in range(1, pages_per_blk):
                tot = tot + sums[b * pages_per_blk + r]
            ksum[b:b + 1, :] = tot

        kmean = ksum[...] * (1.0 / MOBA_BLOCK)
        gate = _dot_nt(qr, kmean, precision=lax.Precision.HIGHEST)
        nio = lax.broadcasted_iota(jnp.int32, gate.shape, 1)
        sel = _topk_rank(gate, nio, (1, n_blk)) < MOBA_TOPK
        kpos = lax.broadcasted_iota(jnp.int32, (N_HEADS, past_len), 1)
        blk_of_key = lax.shift_right_logical(kpos, MOBA_BLOCK.bit_length() - 1)
        chosen = jnp.zeros((N_HEADS, past_len), jnp.bool_)
        for n in range(n_blk):
            chosen = chosen | ((blk_of_key == n) & sel[:, n:n + 1])
        slope = slopes_ref[...]
        s_all = sc[...] * scale - slope * (past_len - kpos).astype(F32)
        s_all = jnp.where(chosen, s_all, MASKED)
        s_self = jnp.sum(qr * kn_ref[...], axis=-1, keepdims=True) * scale
        m = jnp.maximum(jnp.max(s_all, axis=-1, keepdims=True), s_self)
        p = jnp.exp(s_all - m)
        p_self = jnp.exp(s_self - m)
        p_scr[slot] = p
        linv[slot] = jnp.broadcast_to(1.0 / (jnp.sum(p, axis=-1, keepdims=True) + p_self), linv.shape[1:])
        acc0[slot] = p_self * jnp.broadcast_to(vn_ref[...], (N_HEADS, D_MODEL))


def _moba_sample(narrow3, k_new, v_new, cache_k, cache_v, page_table):
    n_seq, n_pages = page_table.shape
    past_len = n_pages * PAGE_SIZE
    assert past_len % MOBA_BLOCK == 0, "the decode token must start a MoBA block"
    n_blk = past_len // MOBA_BLOCK
    page_rows = PAGE_SIZE * N_HEADS

    def k_seq(i):
        return jnp.minimum(i, n_seq - 1)

    def v_seq(i):
        return jnp.maximum(i - 1, 0)

    new_row = pl.BlockSpec((None, 1, D_MODEL), lambda i, pt: (k_seq(i), 0, 0))

    def k_page(g):
        return pl.BlockSpec((page_rows, HEAD_DIM), lambda i, pt: (pt[k_seq(i) * n_pages + g], 0))

    def v_page(g):
        return pl.BlockSpec((page_rows, HEAD_DIM), lambda i, pt: (pt[v_seq(i) * n_pages + g], 0))

    grid_spec = pltpu.PrefetchScalarGridSpec(
        num_scalar_prefetch=1,
        grid=(n_seq + 1,),
        in_specs=[pl.BlockSpec((N_HEADS, 1), lambda i, pt: (0, 0)),
                  pl.BlockSpec((None, None, 1, D_MODEL), lambda i, pt: (NARROW_MQ, k_seq(i), 0, 0)), new_row, new_row]
                 + [k_page(g) for g in range(n_pages)] + [v_page(g) for g in range(n_pages)],
        out_specs=pl.BlockSpec((None, 1, D_MODEL), lambda i, pt: (v_seq(i), 0, 0)),
        scratch_shapes=[pltpu.VMEM((N_HEADS, past_len), F32),
                        pltpu.VMEM((n_blk, D_MODEL), F32),
                        pltpu.VMEM((2, N_HEADS, past_len), F32),
                        pltpu.VMEM((2, N_HEADS, D_MODEL), F32),
                        pltpu.VMEM((2, N_HEADS, HEAD_DIM), F32)])
    return pl.pallas_call(
        functools.partial(_moba_sample_kernel, n_pages=n_pages),
        grid_spec=grid_spec,
        out_shape=jax.ShapeDtypeStruct((n_seq, 1, D_MODEL), F32),
        compiler_params=_params(1),
        name="moba_sample",
    )(page_table.reshape(-1), jnp.asarray(_alibi_slopes()).reshape(N_HEADS, 1),
      narrow3, k_new, v_new, *([cache_k] * n_pages), *([cache_v] * n_pages))


def _layer_norm(x, g, b):
    mu = jnp.mean(x, axis=-1, keepdims=True)
    xc = x - mu
    var = jnp.mean(xc * xc, axis=-1, keepdims=True)
    return xc * lax.rsqrt(var + LN_EPS) * g + b


def _mix_kernel(oa_ref, ob_ref, ga_ref, gb_ref, x_ref, wa_ref, wb_ref, wo_ref, g_ref, b_ref, h_ref):
    ya = _dot(oa_ref[...].astype(BF16), wa_ref[...])
    yb = _dot(ob_ref[...].astype(BF16), wb_ref[...])
    mixed = jax.nn.sigmoid(ga_ref[...].astype(F32)) * ya + jax.nn.sigmoid(gb_ref[...].astype(F32)) * yb
    mix = _dot(mixed.astype(BF16), wo_ref[...])
    h_ref[...] = _layer_norm(DEEPNORM_ALPHA * x_ref[...] + mix, g_ref[...], b_ref[...])


def _mix(o_a, o_b, narrow, x2d, wa, wb, wo, ln_g, ln_b, tm):
    m = x2d.shape[0]
    rows = pl.BlockSpec((tm, D_MODEL), lambda i: (i, 0))
    sec = lambda s: pl.BlockSpec((None, tm, D_MODEL), lambda i, s=s: (s, i, 0))
    full = lambda a: pl.BlockSpec(a.shape, lambda i: (0, 0))
    return pl.pallas_call(
        _mix_kernel,
        grid=(m // tm,),
        in_specs=[rows, rows, sec(NARROW_GA), sec(NARROW_GB), rows, full(wa), full(wb), full(wo), full(ln_g), full(ln_b)],
        out_specs=rows,
        out_shape=jax.ShapeDtypeStruct((m, D_MODEL), F32),
        compiler_params=_params(1),
        name="mix",
    )(o_a, o_b, narrow, narrow, x2d, wa, wb, wo, ln_g, ln_b)


def _mlp_kernel(h_ref, wu_ref, wd_ref, g_ref, b_ref, y_ref, acc, hb):
    j = pl.program_id(1)

    @pl.when(j == 0)
    def _():
        hb[...] = h_ref[...].astype(BF16)
        acc[...] = jnp.zeros(acc.shape, F32)

    u = jnp.maximum(_dot(hb[...], wu_ref[...]), 0.0)
    acc[...] += _dot((u * u).astype(BF16), wd_ref[...])

    @pl.when(j == pl.num_programs(1) - 1)
    def _():
        y_ref[...] = _layer_norm(DEEPNORM_ALPHA * h_ref[...] + acc[...], g_ref[...], b_ref[...])


def _mlp(h, wu, wd, ln_g, ln_b, tm, tf):
    m = h.shape[0]
    d_ff = wu.shape[1]
    rows = pl.BlockSpec((tm, D_MODEL), lambda i, j: (i, 0))
    full = lambda a: pl.BlockSpec(a.shape, lambda i, j: (0, 0))
    return pl.pallas_call(
        _mlp_kernel,
        grid=(m // tm, d_ff // tf),
        in_specs=[rows,
                  pl.BlockSpec((D_MODEL, tf), lambda i, j: (0, j)),
                  pl.BlockSpec((tf, D_MODEL), lambda i, j: (j, 0)),
                  full(ln_g), full(ln_b)],
        out_specs=rows,
        out_shape=jax.ShapeDtypeStruct((m, D_MODEL), F32),
        scratch_shapes=[pltpu.VMEM((tm, D_MODEL), F32), pltpu.VMEM((tm, D_MODEL), BF16)],
        compiler_params=_params(2),
        name="mlp",
    )(h, wu, wd, ln_g, ln_b)


def kernel(x_prompt, x_sample, cache_k, cache_v, state_hgrn, page_table, w_in, lb_param, g_norm_a,
           w_proj_a, w_proj_b, w_out, ln1_g, ln1_b, w_up, w_down, ln2_g, ln2_b):
    assert w_in.shape[0] == DEPTH
    batch, seq, _ = x_prompt.shape
    n_seq, dec_seq, _ = x_sample.shape
    assert dec_seq == 1
    n_pool = cache_k.shape[1]

    w_in_b = w_in[0].astype(BF16)
    wa, wb, wo = w_proj_a[0].astype(BF16), w_proj_b[0].astype(BF16), w_out[0].astype(BF16)
    wu, wd = w_up[0].astype(BF16), w_down[0].astype(BF16)
    gn = g_norm_a[0].reshape(1, HEAD_DIM)
    g1, b1 = ln1_g[0].reshape(1, D_MODEL), ln1_b[0].reshape(1, D_MODEL)
    g2, b2 = ln2_g[0].reshape(1, D_MODEL), ln2_b[0].reshape(1, D_MODEL)

    xp = x_prompt.reshape(batch * seq, D_MODEL)
    wide_p, narrow_p, k_p, v_p = _inproj(xp, w_in_b, tm=1024)
    oa_p, st_p = _hgrn_prompt(wide_p, narrow_p, lb_param, gn, batch, seq, tt=512)
    ob_p = _moba_prompt(narrow_p, k_p, v_p, batch, seq)
    h_p = _mix(oa_p, ob_p, narrow_p, xp, wa, wb, wo, g1, b1, tm=512)
    y_p = _mlp(h_p, wu, wd, g2, b2, tm=1024, tf=1024)

    xs = x_sample.reshape(n_seq, D_MODEL)
    wide_s, narrow_s, k_s, v_s = _inproj(xs, w_in_b, tm=n_seq)
    narrow_s = narrow_s.astype(F32)
    oa_s, st_s = _hgrn_sample(wide_s, narrow_s, lb_param, gn, state_hgrn[0], sb=8)
    ob_s = _moba_sample(narrow_s.reshape(N_NARROW, n_seq, 1, D_MODEL),
                        k_s.reshape(n_seq, 1, D_MODEL), v_s.reshape(n_seq, 1, D_MODEL),
                        cache_k[0].reshape(n_pool * PAGE_SIZE * N_HEADS, HEAD_DIM),
                        cache_v[0].reshape(n_pool * PAGE_SIZE * N_HEADS, HEAD_DIM), page_table)
    h_s = _mix(oa_s, ob_s.reshape(n_seq, D_MODEL), narrow_s, xs, wa, wb, wo, g1, b1, tm=n_seq)
    y_s = _mlp(h_s, wu, wd, g2, b2, tm=n_seq, tf=1024)

    kv_p = (DEPTH, batch, seq, N_HEADS, HEAD_DIM)
    kv_s = (DEPTH, n_seq, dec_seq, N_HEADS, HEAD_DIM)
    return (y_p.reshape(batch, seq, D_MODEL), y_s.reshape(n_seq, dec_seq, D_MODEL),
            k_p.reshape(kv_p), v_p.reshape(kv_p), st_p[None],
            k_s.reshape(kv_s), v_s.reshape(kv_s), st_s[None])
```

```python
import functools

import numpy as np
import jax
import jax.numpy as jnp
from jax import lax
from jax.experimental import pallas as pl
from jax.experimental.pallas import tpu as pltpu

F32 = jnp.float32
BF16 = jnp.bfloat16

D_MODEL = 1024
N_HEADS = 8
HEAD_DIM = 128
N_COLS = 9
COL_MK, COL_MV = 5, 6
N_WIDE = 2
WIDE_Q, WIDE_F = range(N_WIDE)
N_NARROW = 5
NARROW_I, NARROW_G, NARROW_MQ, NARROW_GA, NARROW_GB = range(N_NARROW)
DEPTH = 1
LN_EPS = 1e-5
RMS_EPS = 1e-6
DEEPNORM_ALPHA = (2.0 * DEPTH) ** 0.25
MOBA_BLOCK = 256
MOBA_TOPK = 3
PAGE_SIZE = 128
MASKED = -2e38
M_INIT = -1e38
LOG2E = 1.4426950408889634
SUBLANES = 8
STRIP = 128
HGRN_CHUNK = 128
VMEM_LIMIT = 48 * 1024 * 1024

_NT = (((1,), (1,)), ((), ()))


def _dot(a, b):
    return jnp.dot(a, b, preferred_element_type=F32)


def _dot_nt(a, b, precision=None):
    return lax.dot_general(a, b, _NT, precision=precision, preferred_element_type=F32)


def _params(n_grid):
    return pltpu.CompilerParams(dimension_semantics=("arbitrary",) * n_grid,
                                vmem_limit_bytes=VMEM_LIMIT)


def _inproj_kernel(x_ref, w_ref, wide_ref, narrow_ref, k_ref, v_ref, xb):
    j = pl.program_id(1)

    @pl.when(j == 0)
    def _():
        xb[...] = x_ref[...].astype(BF16)

    y = _dot(xb[...], w_ref[...])

    @pl.when(j < N_WIDE)
    def _():
        wide_ref[...] = y

    @pl.when(j == COL_MK)
    def _():
        k_ref[...] = y

    @pl.when(j == COL_MV)
    def _():
        v_ref[...] = y

    @pl.when((j >= N_WIDE) & (j != COL_MK) & (j != COL_MV))
    def _():
        narrow_ref[...] = y.astype(narrow_ref.dtype)


def _inproj(x2d, w_bf16, tm):
    m = x2d.shape[0]

    def wide(i, j):
        return jnp.minimum(j, N_WIDE - 1), i, 0

    def narrow(i, j):
        sec = jnp.where(j < COL_MK, j - N_WIDE, jnp.where(j > COL_MV, j - N_WIDE - 2, COL_MK - 1 - N_WIDE))
        return jnp.maximum(sec, 0), i, 0

    rows = pl.BlockSpec((tm, D_MODEL), lambda i, j: (i, 0))
    return pl.pallas_call(
        _inproj_kernel,
        grid=(m // tm, N_COLS),
        in_specs=[rows, pl.BlockSpec((D_MODEL, D_MODEL), lambda i, j: (0, j))],
        out_specs=[pl.BlockSpec((None, tm, D_MODEL), wide), pl.BlockSpec((None, tm, D_MODEL), narrow), rows, rows],
        out_shape=[jax.ShapeDtypeStruct((N_WIDE, m, D_MODEL), F32),
                   jax.ShapeDtypeStruct((N_NARROW, m, D_MODEL), BF16),
                   jax.ShapeDtypeStruct((m, D_MODEL), F32),
                   jax.ShapeDtypeStruct((m, D_MODEL), F32)],
        scratch_shapes=[pltpu.VMEM((tm, D_MODEL), BF16)],
        compiler_params=_params(2),
        name="inproj",
    )(x2d, w_bf16)


def _lower_bound(p):
    m = p[0:1]
    for r in range(1, p.shape[0]):
        m = jnp.maximum(m, p[r:r + 1])
    e = [jnp.exp(p[r:r + 1] - m) for r in range(p.shape[0])]
    tot = e[0]
    for r in range(1, p.shape[0]):
        tot = tot + e[r]
    return e[0] / tot


def _gates_from_z(z, lb):
    e = jnp.exp(-jnp.abs(z))
    r = 1.0 / (1.0 + e)
    er = e * r
    pos = z >= 0
    oml = 1.0 - lb
    f = lb + oml * jnp.where(pos, r, er)
    k = oml * jnp.where(pos, er, r)
    return f, k


def _bc8(row):
    return jnp.broadcast_to(row, (SUBLANES, row.shape[-1]))


def _tile_levels(x, q, k, row_masks):
    ninf = -jnp.inf
    sides = []
    odd, upper_pair, upper_half = row_masks
    up = odd
    tot = pltpu.roll(x, 1, 0)
    sides.append((q * jnp.exp2(jnp.where(up, x, ninf)), jnp.where(up, 0.0, k)))
    sc = x + jnp.where(up, tot, 0.0)
    up = upper_pair
    tot = jnp.where(upper_half, _bc8(sc[5:6]), _bc8(sc[1:2]))
    sides.append((q * jnp.exp2(jnp.where(up, sc, ninf)), k * jnp.exp2(jnp.where(up, ninf, tot - sc))))
    sc = sc + jnp.where(up, tot, 0.0)
    up = upper_half
    tot = _bc8(sc[3:4])
    sides.append((q * jnp.exp2(jnp.where(up, sc, ninf)), k * jnp.exp2(jnp.where(up, ninf, tot - sc))))
    sc = sc + jnp.where(up, tot, 0.0)
    return sc, sides


def _hgrn_sides(qa, z, lb):
    n_tiles = HGRN_CHUNK // SUBLANES
    f, k = _gates_from_z(z, lb)
    logf = jnp.log2(f)
    q = qa * jax.nn.sigmoid(qa)

    def tiles(a):
        return [a[SUBLANES * j:SUBLANES * (j + 1)] for j in range(n_tiles)]

    qt, kt = tiles(q), tiles(k)
    r8 = lax.broadcasted_iota(jnp.int32, (SUBLANES, HEAD_DIM), 0)
    row_masks = ((r8 & 1) == 1, (r8 & 2) == 2, r8 >= 4)
    in_tile = [_tile_levels(x, qt[j], kt[j], row_masks) for j, x in enumerate(tiles(logf))]
    levels = [(2 << b, jnp.concatenate([in_tile[j][1][b][0] for j in range(n_tiles)], axis=0),
               jnp.concatenate([in_tile[j][1][b][1] for j in range(n_tiles)], axis=0)) for b in range(3)]

    sc = [t[0] for t in in_tile]
    zero = jnp.zeros((SUBLANES, HEAD_DIM), F32)
    nb = 1
    while nb < n_tiles:
        ql = [zero] * n_tiles
        kl = [zero] * n_tiles
        new = list(sc)
        for m in range(n_tiles // (2 * nb)):
            lo = range(2 * m * nb, (2 * m + 1) * nb)
            up = range((2 * m + 1) * nb, (2 * m + 2) * nb)
            tot = _bc8(sc[lo[-1]][SUBLANES - 1:SUBLANES])
            for j in lo:
                kl[j] = kt[j] * jnp.exp2(tot - sc[j])
            for j in up:
                ql[j] = qt[j] * jnp.exp2(sc[j])
                new[j] = sc[j] + tot
        levels.append((2 * nb * SUBLANES, jnp.concatenate(ql, axis=0), jnp.concatenate(kl, axis=0)))
        sc = new
        nb *= 2
    a_t = sc
    a_last = a_t[n_tiles - 1][SUBLANES - 1:SUBLANES]
    qg = jnp.concatenate([qt[j] * jnp.exp2(a_t[j]) for j in range(n_tiles)], axis=0)
    kd = jnp.concatenate([kt[j] * jnp.exp2(_bc8(a_last) - a_t[j]) for j in range(n_tiles)], axis=0)
    diag = jnp.sum(q * k, axis=-1, keepdims=True)
    return ([(group, ql.astype(BF16), kl.astype(BF16)) for group, ql, kl in levels],
            diag, qg.astype(BF16), kd.astype(BF16), jnp.exp2(a_last))


def _hgrn_mix(sides, v, ga, gnorm, st):
    levels, diag, qg, kd, chunk_decay = sides
    row = lax.broadcasted_iota(jnp.int32, (HGRN_CHUNK, HGRN_CHUNK), 0)
    col = lax.broadcasted_iota(jnp.int32, (HGRN_CHUNK, HGRN_CHUNK), 1)
    differ = row ^ col
    scores = jnp.where(differ == 0, diag, 0.0)
    for group, ql, kl in levels:
        x = _dot_nt(ql, kl)
        if group < HGRN_CHUNK:
            x = jnp.where(differ < group, x, 0.0)
        scores = scores + x

    o = _dot_nt(qg, st.astype(BF16)) + _dot(scores.astype(BF16), v.astype(BF16))
    st_new = st * chunk_decay + _dot(v.T.astype(BF16), kd)

    ms = jnp.mean(o * o, axis=-1, keepdims=True)
    o = o * lax.rsqrt(ms + RMS_EPS) * gnorm * (ga * jax.nn.sigmoid(ga))
    return o, st_new


def _hgrn_prompt_kernel(q_ref, f_ref, i_ref, g_ref, lbp_ref, gn_ref, o_ref, s_ref, st_scr, *, n_chunks):
    t = pl.program_id(1)

    @pl.when(t == 0)
    def _():
        st_scr[...] = jnp.zeros(st_scr.shape, F32)

    gnorm = gn_ref[...]

    def chunk_body(c, carry):
        rows = pl.ds(pl.multiple_of(c * HGRN_CHUNK, HGRN_CHUNK), HGRN_CHUNK)

        def head_lanes(h):
            return slice(h * HEAD_DIM, (h + 1) * HEAD_DIM)

        def sides_of(h):
            lanes = head_lanes(h)
            return _hgrn_sides(q_ref[rows, lanes], f_ref[rows, lanes], _lower_bound(lbp_ref[:, lanes]))

        sides = sides_of(0)
        for h in range(N_HEADS):
            lanes = head_lanes(h)
            nxt = sides_of(h + 1) if h + 1 < N_HEADS else None
            o, st_new = _hgrn_mix(sides, i_ref[rows, lanes].astype(F32), g_ref[rows, lanes].astype(F32),
                                  gnorm, st_scr[h])
            st_scr[h] = st_new
            o_ref[rows, lanes] = o.astype(o_ref.dtype)
            sides = nxt
        return carry

    lax.fori_loop(0, n_chunks, chunk_body, 0)

    @pl.when(t == pl.num_programs(1) - 1)
    def _():
        for h in range(N_HEADS):
            s_ref[h] = st_scr[h].T


def _hgrn_prompt(wide, narrow, lb_param, g_norm, batch, seq, tt):
    n_t = seq // tt

    def sec(s):
        return pl.BlockSpec((None, tt, D_MODEL), lambda b, t, s=s: (s, b * n_t + t, 0))

    return pl.pallas_call(
        functools.partial(_hgrn_prompt_kernel, n_chunks=tt // HGRN_CHUNK),
        grid=(batch, n_t),
        in_specs=[sec(WIDE_Q), sec(WIDE_F), sec(NARROW_I), sec(NARROW_G),
                  pl.BlockSpec(lb_param.shape, lambda b, t: (0, 0)),
                  pl.BlockSpec(g_norm.shape, lambda b, t: (0, 0))],
        out_specs=[pl.BlockSpec((tt, D_MODEL), lambda b, t: (b * n_t + t, 0)),
                   pl.BlockSpec((None, N_HEADS, HEAD_DIM, HEAD_DIM), lambda b, t: (b, 0, 0, 0))],
        out_shape=[jax.ShapeDtypeStruct((batch * seq, D_MODEL), BF16),
                   jax.ShapeDtypeStruct((batch, N_HEADS, HEAD_DIM, HEAD_DIM), F32)],
        scratch_shapes=[pltpu.VMEM((N_HEADS, HEAD_DIM, HEAD_DIM), F32)],
        compiler_params=_params(2),
        name="hgrn_prompt",
    )(wide, wide, narrow, narrow, lb_param, g_norm)


def _hgrn_sample_kernel(q_ref, f_ref, i_ref, g_ref, lbp_ref, gn_ref, s0_ref, o_ref, s_ref):
    sb = q_ref.shape[0]
    lb = _lower_bound(lbp_ref[...])
    f, k = _gates_from_z(f_ref[...], lb)
    qa = q_ref[...]
    q = qa * jax.nn.sigmoid(qa)
    ga = g_ref[...]
    gate = ga * jax.nn.sigmoid(ga)
    v = i_ref[...]
    gnorm = gn_ref[...]
    qf = q * f
    qk = q * k
    for h in range(N_HEADS):
        hl = slice(h * HEAD_DIM, (h + 1) * HEAD_DIM)
        f_cols = f[:, hl].T
        k_cols = k[:, hl].T
        for b in range(sb):
            s0 = s0_ref[b, h]
            v_row = v[b:b + 1, hl]
            s_ref[b, h] = s0 * f_cols[:, b:b + 1] + k_cols[:, b:b + 1] * v_row
            o = _dot(qf[b:b + 1, hl].astype(BF16), s0.astype(BF16)) \
                + jnp.sum(qk[b:b + 1, hl], axis=-1, keepdims=True) * v_row
            ms = jnp.mean(o * o, axis=-1, keepdims=True)
            o_ref[b:b + 1, hl] = o * lax.rsqrt(ms + RMS_EPS) * gnorm * gate[b:b + 1, hl]


def _hgrn_sample(wide, narrow, lb_param, g_norm, state, sb):
    n = state.shape[0]

    def sec(s):
        return pl.BlockSpec((None, sb, D_MODEL), lambda i, s=s: (s, i, 0))

    st_spec = pl.BlockSpec((sb, N_HEADS, HEAD_DIM, HEAD_DIM), lambda i: (i, 0, 0, 0))
    return pl.pallas_call(
        _hgrn_sample_kernel,
        grid=(n // sb,),
        in_specs=[sec(WIDE_Q), sec(WIDE_F), sec(NARROW_I), sec(NARROW_G),
                  pl.BlockSpec(lb_param.shape, lambda i: (0, 0)),
                  pl.BlockSpec(g_norm.shape, lambda i: (0, 0)),
                  st_spec],
        out_specs=[pl.BlockSpec((sb, D_MODEL), lambda i: (i, 0)), st_spec],
        out_shape=[jax.ShapeDtypeStruct((n, D_MODEL), F32),
                   jax.ShapeDtypeStruct(state.shape, F32)],
        compiler_params=_params(1),
        name="hgrn_sample",
    )(wide, wide, narrow, narrow, lb_param, g_norm, state)


def _alibi_slopes():
    return np.asarray([2.0 ** (-8.0 * (h + 1) / N_HEADS) for h in range(N_HEADS)], np.float32)


def _topk_rank(g, idx, n):
    axis, count = n
    rank = jnp.zeros(g.shape, F32)
    for m in range(count):
        gm = lax.slice_in_dim(g, m, m + 1, axis=axis)
        rank = rank + ((gm > g) | ((gm == g) & (idx > m))).astype(F32)
    return rank


def _moba_prompt_kernel(slopes_ref, q_ref, k_ref, v_ref, o_ref,
                        kb, vt, qtb, kmean, bias, dist, *, n_blk):
    h = pl.program_id(1)
    blk = MOBA_BLOCK
    n_pairs = n_blk // 2
    slope2 = slopes_ref[h] * LOG2E
    scale2 = HEAD_DIM ** -0.5 * LOG2E

    def blk_rows(n):
        return slice(n * blk, (n + 1) * blk)

    def prep(n, carry):
        rows = blk_rows(n)
        kk = k_ref[rows, :]
        kb[rows, :] = kk.astype(BF16)
        kmean[pl.ds(n, 1), :] = jnp.sum(kk, axis=0, keepdims=True) * (1.0 / blk)
        vt[:, rows] = v_ref[rows, :].T.astype(BF16)
        return carry

    for n in range(n_blk):
        prep(n, 0)

    key = lax.broadcasted_iota(jnp.int32, (blk, blk), 0)
    qry = lax.broadcasted_iota(jnp.int32, (blk, blk), 1)
    d0 = slope2 * (qry - key).astype(F32)
    dist[0] = d0
    dist[1] = jnp.where(key <= qry, d0, -MASKED)

    def choose(qi, carry):
        rows = blk_rows(qi)
        q_t = q_ref[rows, :].astype(F32).T
        qtb[:, rows] = (q_t * scale2).astype(BF16)
        gate = jnp.dot(kmean[...], q_t, precision=lax.Precision.HIGHEST, preferred_element_type=F32)
        nio = lax.broadcasted_iota(jnp.int32, gate.shape, 0)
        past = nio < qi
        g = jnp.where(past, gate, -jnp.inf)
        attend = (past & (_topk_rank(g, nio, (0, n_blk)) < MOBA_TOPK)) | (nio == qi)
        bias[:, rows] = jnp.where(attend, -slope2 * ((qi - nio) * blk).astype(F32), MASKED)
        return carry

    for qi in range(n_blk):
        choose(qi, 0)

    steps = [(j, i) for j in range(n_pairs) for i in range(j + 1)]

    def pair(n):
        return slice(n * 2 * blk, (n + 1) * 2 * blk)

    def raw_scores(j, i):
        return _dot(kb[pair(i), :], qtb[:, pair(j)])

    state = {}

    def flush(j, i, p, alpha):
        m, l, acc = state[j]
        acc = acc * alpha + _dot(vt[:, pair(i)], p)
        state[j] = (m, l, acc)
        if i == j:
            o_ref[pair(j), :] = (acc * (1.0 / l)).T.astype(o_ref.dtype)
            del state[j]

    s_next = raw_scores(*steps[0])
    pending = None
    for n, (j, i) in enumerate(steps):
        s = s_next
        if n + 1 < len(steps):
            s_next = raw_scores(*steps[n + 1])
        if pending is not None:
            flush(*pending)
        if i == 0:
            state[j] = (jnp.full((1, 2 * blk), M_INIT, F32), jnp.zeros((1, 2 * blk), F32),
                        jnp.zeros((HEAD_DIM, 2 * blk), F32))
        m, l, acc = state[j]
        own = 1 if i == j else 0
        ms, ls, alphas, ps = [], [], [], []
        for t in range(2 * blk // STRIP):
            lanes = slice(t * STRIP, (t + 1) * STRIP)
            in_blk = slice(t * STRIP % blk, t * STRIP % blk + STRIP)
            first = t * STRIP < blk
            xa = s[:blk, lanes] - dist[own if first else 0, :, in_blk]
            xb = s[blk:, lanes] - dist[0 if first else own, :, in_blk]
            ba = bias[2 * i:2 * i + 1, j * 2 * blk + t * STRIP:j * 2 * blk + (t + 1) * STRIP]
            bb = bias[2 * i + 1:2 * i + 2, j * 2 * blk + t * STRIP:j * 2 * blk + (t + 1) * STRIP]
            m_t = jnp.maximum(m[:, lanes], jnp.maximum(jnp.max(xa, axis=0, keepdims=True) + ba,
                                                       jnp.max(xb, axis=0, keepdims=True) + bb))
            alpha_t = jnp.exp2(m[:, lanes] - m_t)
            pa = jnp.exp2(xa - (m_t - ba))
            pb = jnp.exp2(xb - (m_t - bb))
            ls.append(alpha_t * l[:, lanes] + jnp.sum(pa, axis=0, keepdims=True)
                      + jnp.sum(pb, axis=0, keepdims=True))
            ms.append(m_t)
            alphas.append(alpha_t)
            ps.append(jnp.concatenate([pa.astype(BF16), pb.astype(BF16)], axis=0))
        state[j] = (jnp.concatenate(ms, axis=1), jnp.concatenate(ls, axis=1), acc)
        pending = (j, i, jnp.concatenate(ps, axis=1), jnp.concatenate(alphas, axis=1))
    flush(*pending)


def _moba_prompt(narrow, k_rows, v_rows, batch, seq):
    n_blk = seq // MOBA_BLOCK
    assert n_blk % 2 == 0, "key blocks are processed in pairs"
    head = pl.BlockSpec((seq, HEAD_DIM), lambda b, h: (b, h))
    return pl.pallas_call(
        functools.partial(_moba_prompt_kernel, n_blk=n_blk),
        grid=(batch, N_HEADS),
        in_specs=[pl.BlockSpec(memory_space=pltpu.SMEM),
                  pl.BlockSpec((None, seq, HEAD_DIM), lambda b, h: (NARROW_MQ, b, h)),
                  head, head],
        out_specs=head,
        out_shape=jax.ShapeDtypeStruct((batch * seq, D_MODEL), BF16),
        scratch_shapes=[pltpu.VMEM((seq, HEAD_DIM), BF16),
                        pltpu.VMEM((HEAD_DIM, seq), BF16),
                        pltpu.VMEM((HEAD_DIM, seq), BF16),
                        pltpu.VMEM((n_blk, HEAD_DIM), F32),
                        pltpu.VMEM((n_blk, seq), F32),
                        pltpu.VMEM((2, MOBA_BLOCK, MOBA_BLOCK), F32)],
        compiler_params=_params(2),
        name="moba_prompt",
    )(jnp.asarray(_alibi_slopes()), narrow, k_rows, v_rows)


def _page_heads_on_lanes(page_ref):
    return jnp.concatenate([page_ref[pl.ds(h, PAGE_SIZE, stride=N_HEADS), :] for h in range(N_HEADS)], axis=1)


def _moba_sample_kernel(pt_ref, slopes_ref, q_ref, kn_ref, vn_ref, *rest, n_pages):
    del pt_ref
    k_refs, v_refs = rest[:n_pages], rest[n_pages:2 * n_pages]
    o_ref, sc, ksum, p_scr, acc0, linv = rest[2 * n_pages:]
    i = pl.program_id(0)
    n_seq = pl.num_programs(0) - 1
    past_len = n_pages * PAGE_SIZE
    n_blk = past_len // MOBA_BLOCK
    pages_per_blk = MOBA_BLOCK // PAGE_SIZE
    scale = HEAD_DIM ** -0.5

    @pl.when(i > 0)
    def _():
        slot = lax.rem(i - 1, 2)
        a = acc0[slot]
        for g in range(n_pages):
            pj = p_scr[slot, :, g * PAGE_SIZE:(g + 1) * PAGE_SIZE]
            a = a + _dot(pj.astype(BF16), _page_heads_on_lanes(v_refs[g]).astype(BF16))
        a = a * linv[slot][:, 0:1]
        for h in range(N_HEADS):
            hl = slice(h * HEAD_DIM, (h + 1) * HEAD_DIM)
            o_ref[:, hl] = a[h:h + 1, hl]

    @pl.when(i < n_seq)
    def _():
        slot = lax.rem(i, 2)
        q = q_ref[...].astype(F32)
        head_of_lane = lax.shift_right_logical(
            lax.broadcasted_iota(jnp.int32, (N_HEADS, D_MODEL), 1), HEAD_DIM.bit_length() - 1)
        head_of_row = lax.broadcasted_iota(jnp.int32, (N_HEADS, D_MODEL), 0)
        qr = jnp.where(head_of_lane == head_of_row, jnp.broadcast_to(q, (N_HEADS, D_MODEL)), 0.0)
        qrb = qr.astype(BF16)
        sums = []
        for g in range(n_pages):
            kp = _page_heads_on_lanes(k_refs[g])
            sc[:, g * PAGE_SIZE:(g + 1) * PAGE_SIZE] = _dot_nt(qrb, kp.astype(BF16))
            sums.append(jnp.sum(kp, axis=0, keepdims=True))
        for b in range(n_blk):
            tot = sums[b * pages_per_blk]
            for r in range(1, pages_per_blk):
                tot = tot + sums[b * pages_per_blk + r]
            ksum[b:b + 1, :] = tot

        kmean = ksum[...] * (1.0 / MOBA_BLOCK)
        gate = _dot_nt(qr, kmean, precision=lax.Precision.HIGHEST)
        nio = lax.broadcasted_iota(jnp.int32, gate.shape, 1)
        sel = _topk_rank(gate, nio, (1, n_blk)) < MOBA_TOPK
        kpos = lax.broadcasted_iota(jnp.int32, (N_HEADS, past_len), 1)
        blk_of_key = lax.shift_right_logical(kpos, MOBA_BLOCK.bit_length() - 1)
        chosen = jnp.zeros((N_HEADS, past_len), jnp.bool_)
        for n in range(n_blk):
            chosen = chosen | ((blk_of_key == n) & sel[:, n:n + 1])
        slope = slopes_ref[...]
        s_all = sc[...] * scale - slope * (past_len - kpos).astype(F32)
        s_all = jnp.where(chosen, s_all, MASKED)
        s_self = jnp.sum(qr * kn_ref[...], axis=-1, keepdims=True) * scale
        m = jnp.maximum(jnp.max(s_all, axis=-1, keepdims=True), s_self)
        p = jnp.exp(s_all - m)
        p_self = jnp.exp(s_self - m)
        p_scr[slot] = p
        linv[slot] = jnp.broadcast_to(1.0 / (jnp.sum(p, axis=-1, keepdims=True) + p_self), linv.shape[1:])
        acc0[slot] = p_self * jnp.broadcast_to(vn_ref[...], (N_HEADS, D_MODEL))


def _moba_sample(narrow3, k_new, v_new, cache_k, cache_v, page_table):
    n_seq, n_pages = page_table.shape
    past_len = n_pages * PAGE_SIZE
    assert past_len % MOBA_BLOCK == 0, "the decode token must start a MoBA block"
    n_blk = past_len // MOBA_BLOCK
    page_rows = PAGE_SIZE * N_HEADS

    def k_seq(i):
        return jnp.minimum(i, n_seq - 1)

    def v_seq(i):
        return jnp.maximum(i - 1, 0)

    new_row = pl.BlockSpec((None, 1, D_MODEL), lambda i, pt: (k_seq(i), 0, 0))

    def k_page(g):
        return pl.BlockSpec((page_rows, HEAD_DIM), lambda i, pt: (pt[k_seq(i) * n_pages + g], 0))

    def v_page(g):
        return pl.BlockSpec((page_rows, HEAD_DIM), lambda i, pt: (pt[v_seq(i) * n_pages + g], 0))

    grid_spec = pltpu.PrefetchScalarGridSpec(
        num_scalar_prefetch=1,
        grid=(n_seq + 1,),
        in_specs=[pl.BlockSpec((N_HEADS, 1), lambda i, pt: (0, 0)),
                  pl.BlockSpec((None, None, 1, D_MODEL), lambda i, pt: (NARROW_MQ, k_seq(i), 0, 0)), new_row, new_row]
                 + [k_page(g) for g in range(n_pages)] + [v_page(g) for g in range(n_pages)],
        out_specs=pl.BlockSpec((None, 1, D_MODEL), lambda i, pt: (v_seq(i), 0, 0)),
        scratch_shapes=[pltpu.VMEM((N_HEADS, past_len), F32),
                        pltpu.VMEM((n_blk, D_MODEL), F32),
                        pltpu.VMEM((2, N_HEADS, past_len), F32),
                        pltpu.VMEM((2, N_HEADS, D_MODEL), F32),
                        pltpu.VMEM((2, N_HEADS, HEAD_DIM), F32)])
    return pl.pallas_call(
        functools.partial(_moba_sample_kernel, n_pages=n_pages),
        grid_spec=grid_spec,
        out_shape=jax.ShapeDtypeStruct((n_seq, 1, D_MODEL), F32),
        compiler_params=_params(1),
        name="moba_sample",
    )(page_table.reshape(-1), jnp.asarray(_alibi_slopes()).reshape(N_HEADS, 1),
      narrow3, k_new, v_new, *([cache_k] * n_pages), *([cache_v] * n_pages))


def _layer_norm(x, g, b):
    mu = jnp.mean(x, axis=-1, keepdims=True)
    xc = x - mu
    var = jnp.mean(xc * xc, axis=-1, keepdims=True)
    return xc * lax.rsqrt(var + LN_EPS) * g + b


def _mix_kernel(oa_ref, ob_ref, ga_ref, gb_ref, x_ref, wa_ref, wb_ref, wo_ref, g_ref, b_ref, h_ref):
    ya = _dot(oa_ref[...].astype(BF16), wa_ref[...])
    yb = _dot(ob_ref[...].astype(BF16), wb_ref[...])
    mixed = jax.nn.sigmoid(ga_ref[...].astype(F32)) * ya + jax.nn.sigmoid(gb_ref[...].astype(F32)) * yb
    mix = _dot(mixed.astype(BF16), wo_ref[...])
    h_ref[...] = _layer_norm(DEEPNORM_ALPHA * x_ref[...] + mix, g_ref[...], b_ref[...])


def _mix(o_a, o_b, narrow, x2d, wa, wb, wo, ln_g, ln_b, tm):
    m = x2d.shape[0]
    rows = pl.BlockSpec((tm, D_MODEL), lambda i: (i, 0))
    sec = lambda s: pl.BlockSpec((None, tm, D_MODEL), lambda i, s=s: (s, i, 0))
    full = lambda a: pl.BlockSpec(a.shape, lambda i: (0, 0))
    return pl.pallas_call(
        _mix_kernel,
        grid=(m // tm,),
        in_specs=[rows, rows, sec(NARROW_GA), sec(NARROW_GB), rows, full(wa), full(wb), full(wo), full(ln_g), full(ln_b)],
        out_specs=rows,
        out_shape=jax.ShapeDtypeStruct((m, D_MODEL), F32),
        compiler_params=_params(1),
        name="mix",
    )(o_a, o_b, narrow, narrow, x2d, wa, wb, wo, ln_g, ln_b)


def _mlp_kernel(h_ref, wu_ref, wd_ref, g_ref, b_ref, y_ref, acc, hb):
    j = pl.program_id(1)

    @pl.when(j == 0)
    def _():
        hb[...] = h_ref[...].astype(BF16)
        acc[...] = jnp.zeros(acc.shape, F32)

    u = jnp.maximum(_dot(hb[...], wu_ref[...]), 0.0)
    acc[...] += _dot((u * u).astype(BF16), wd_ref[...])

    @pl.when(j == pl.num_programs(1) - 1)
    def _():
        y_ref[...] = _layer_norm(DEEPNORM_ALPHA * h_ref[...] + acc[...], g_ref[...], b_ref[...])


def _mlp(h, wu, wd, ln_g, ln_b, tm, tf):
    m = h.shape[0]
    d_ff = wu.shape[1]
    rows = pl.BlockSpec((tm, D_MODEL), lambda i, j: (i, 0))
    full = lambda a: pl.BlockSpec(a.shape, lambda i, j: (0, 0))
    return pl.pallas_call(
        _mlp_kernel,
        grid=(m // tm, d_ff // tf),
        in_specs=[rows,
                  pl.BlockSpec((D_MODEL, tf), lambda i, j: (0, j)),
                  pl.BlockSpec((tf, D_MODEL), lambda i, j: (j, 0)),
                  full(ln_g), full(ln_b)],
        out_specs=rows,
        out_shape=jax.ShapeDtypeStruct((m, D_MODEL), F32),
        scratch_shapes=[pltpu.VMEM((tm, D_MODEL), F32), pltpu.VMEM((tm, D_MODEL), BF16)],
        compiler_params=_params(2),
        name="mlp",
    )(h, wu, wd, ln_g, ln_b)


def kernel(x_prompt, x_sample, cache_k, cache_v, state_hgrn, page_table, w_in, lb_param, g_norm_a,
           w_proj_a, w_proj_b, w_out, ln1_g, ln1_b, w_up, w_down, ln2_g, ln2_b):
    assert w_in.shape[0] == DEPTH
    batch, seq, _ = x_prompt.shape
    n_seq, dec_seq, _ = x_sample.shape
    assert dec_seq == 1
    n_pool = cache_k.shape[1]

    w_in_b = w_in[0].astype(BF16)
    wa, wb, wo = w_proj_a[0].astype(BF16), w_proj_b[0].astype(BF16), w_out[0].astype(BF16)
    wu, wd = w_up[0].astype(BF16), w_down[0].astype(BF16)
    gn = g_norm_a[0].reshape(1, HEAD_DIM)
    g1, b1 = ln1_g[0].reshape(1, D_MODEL), ln1_b[0].reshape(1, D_MODEL)
    g2, b2 = ln2_g[0].reshape(1, D_MODEL), ln2_b[0].reshape(1, D_MODEL)

    xp = x_prompt.reshape(batch * seq, D_MODEL)
    wide_p, narrow_p, k_p, v_p = _inproj(xp, w_in_b, tm=1024)
    oa_p, st_p = _hgrn_prompt(wide_p, narrow_p, lb_param, gn, batch, seq, tt=512)
    ob_p = _moba_prompt(narrow_p, k_p, v_p, batch, seq)
    h_p = _mix(oa_p, ob_p, narrow_p, xp, wa, wb, wo, g1, b1, tm=512)
    y_p = _mlp(h_p, wu, wd, g2, b2, tm=1024, tf=1024)

    xs = x_sample.reshape(n_seq, D_MODEL)
    wide_s, narrow_s, k_s, v_s = _inproj(xs, w_in_b, tm=n_seq)
    narrow_s = narrow_s.astype(F32)
    oa_s, st_s = _hgrn_sample(wide_s, narrow_s, lb_param, gn, state_hgrn[0], sb=8)
    ob_s = _moba_sample(narrow_s.reshape(N_NARROW, n_seq, 1, D_MODEL),
                        k_s.reshape(n_seq, 1, D_MODEL), v_s.reshape(n_seq, 1, D_MODEL),
                        cache_k[0].reshape(n_pool * PAGE_SIZE * N_HEADS, HEAD_DIM),
                        cache_v[0].reshape(n_pool * PAGE_SIZE * N_HEADS, HEAD_DIM), page_table)
    h_s = _mix(oa_s, ob_s.reshape(n_seq, D_MODEL), narrow_s, xs, wa, wb, wo, g1, b1, tm=n_seq)
    y_s = _mlp(h_s, wu, wd, g2, b2, tm=n_seq, tf=1024)

    kv_p = (DEPTH, batch, seq, N_HEADS, HEAD_DIM)
    kv_s = (DEPTH, n_seq, dec_seq, N_HEADS, HEAD_DIM)
    return (y_p.reshape(batch, seq, D_MODEL), y_s.reshape(n_seq, dec_seq, D_MODEL),
            k_p.reshape(kv_p), v_p.reshape(kv_p), st_p[None],
            k_s.reshape(kv_s), v_s.reshape(kv_s), st_s[None])
```

```python
import functools

import numpy as np
import jax
import jax.numpy as jnp
from jax import lax
from jax.experimental import pallas as pl
from jax.experimental.pallas import tpu as pltpu

F32 = jnp.float32
BF16 = jnp.bfloat16

D_MODEL = 1024
N_HEADS = 8
HEAD_DIM = 128
N_COLS = 9
COL_MK, COL_MV = 5, 6
N_WIDE = 2
WIDE_Q, WIDE_F = range(N_WIDE)
N_NARROW = 5
NARROW_I, NARROW_G, NARROW_MQ, NARROW_GA, NARROW_GB = range(N_NARROW)
DEPTH = 1
LN_EPS = 1e-5
RMS_EPS = 1e-6
DEEPNORM_ALPHA = (2.0 * DEPTH) ** 0.25
MOBA_BLOCK = 256
MOBA_TOPK = 3
PAGE_SIZE = 128
MASKED = -2e38
M_INIT = -1e38
LOG2E = 1.4426950408889634
SUBLANES = 8
STRIP = 128
HGRN_CHUNK = 128
VMEM_LIMIT = 48 * 1024 * 1024

_NT = (((1,), (1,)), ((), ()))


def _dot(a, b):
    return jnp.dot(a, b, preferred_element_type=F32)


def _dot_nt(a, b, precision=None):
    return lax.dot_general(a, b, _NT, precision=precision, preferred_element_type=F32)


def _params(n_grid):
    return pltpu.CompilerParams(dimension_semantics=("arbitrary",) * n_grid,
                                vmem_limit_bytes=VMEM_LIMIT)


def _inproj_kernel(x_ref, w_ref, wide_ref, narrow_ref, k_ref, v_ref, xb):
    j = pl.program_id(1)

    @pl.when(j == 0)
    def _():
        xb[...] = x_ref[...].astype(BF16)

    y = _dot(xb[...], w_ref[...])

    @pl.when(j < N_WIDE)
    def _():
        wide_ref[...] = y

    @pl.when(j == COL_MK)
    def _():
        k_ref[...] = y

    @pl.when(j == COL_MV)
    def _():
        v_ref[...] = y

    @pl.when((j >= N_WIDE) & (j != COL_MK) & (j != COL_MV))
    def _():
        narrow_ref[...] = y.astype(narrow_ref.dtype)


def _inproj(x2d, w_bf16, tm):
    m = x2d.shape[0]

    def wide(i, j):
        return jnp.minimum(j, N_WIDE - 1), i, 0

    def narrow(i, j):
        sec = jnp.where(j < COL_MK, j - N_WIDE, jnp.where(j > COL_MV, j - N_WIDE - 2, COL_MK - 1 - N_WIDE))
        return jnp.maximum(sec, 0), i, 0

    rows = pl.BlockSpec((tm, D_MODEL), lambda i, j: (i, 0))
    return pl.pallas_call(
        _inproj_kernel,
        grid=(m // tm, N_COLS),
        in_specs=[rows, pl.BlockSpec((D_MODEL, D_MODEL), lambda i, j: (0, j))],
        out_specs=[pl.BlockSpec((None, tm, D_MODEL), wide), pl.BlockSpec((None, tm, D_MODEL), narrow), rows, rows],
        out_shape=[jax.ShapeDtypeStruct((N_WIDE, m, D_MODEL), F32),
                   jax.ShapeDtypeStruct((N_NARROW, m, D_MODEL), BF16),
                   jax.ShapeDtypeStruct((m, D_MODEL), F32),
                   jax.ShapeDtypeStruct((m, D_MODEL), F32)],
        scratch_shapes=[pltpu.VMEM((tm, D_MODEL), BF16)],
        compiler_params=_params(2),
        name="inproj",
    )(x2d, w_bf16)


def _lower_bound(p):
    m = p[0:1]
    for r in range(1, p.shape[0]):
        m = jnp.maximum(m, p[r:r + 1])
    e = [jnp.exp(p[r:r + 1] - m) for r in range(p.shape[0])]
    tot = e[0]
    for r in range(1, p.shape[0]):
        tot = tot + e[r]
    return e[0] / tot


def _gates_from_z(z, lb):
    e = jnp.exp(-jnp.abs(z))
    r = 1.0 / (1.0 + e)
    er = e * r
    pos = z >= 0
    oml = 1.0 - lb
    f = lb + oml * jnp.where(pos, r, er)
    k = oml * jnp.where(pos, er, r)
    return f, k


def _bc8(row):
    return jnp.broadcast_to(row, (SUBLANES, row.shape[-1]))


def _tile_levels(x, q, k, row_masks):
    ninf = -jnp.inf
    sides = []
    odd, upper_pair, upper_half = row_masks
    up = odd
    tot = pltpu.roll(x, 1, 0)
    sides.append((q * jnp.exp2(jnp.where(up, x, ninf)), jnp.where(up, 0.0, k)))
    sc = x + jnp.where(up, tot, 0.0)
    up = upper_pair
    tot = jnp.where(upper_half, _bc8(sc[5:6]), _bc8(sc[1:2]))
    sides.append((q * jnp.exp2(jnp.where(up, sc, ninf)), k * jnp.exp2(jnp.where(up, ninf, tot - sc))))
    sc = sc + jnp.where(up, tot, 0.0)
    up = upper_half
    tot = _bc8(sc[3:4])
    sides.append((q * jnp.exp2(jnp.where(up, sc, ninf)), k * jnp.exp2(jnp.where(up, ninf, tot - sc))))
    sc = sc + jnp.where(up, tot, 0.0)
    return sc, sides


def _hgrn_sides(qa, z, lb):
    n_tiles = HGRN_CHUNK // SUBLANES
    f, k = _gates_from_z(z, lb)
    logf = jnp.log2(f)
    q = qa * jax.nn.sigmoid(qa)

    def tiles(a):
        return [a[SUBLANES * j:SUBLANES * (j + 1)] for j in range(n_tiles)]

    qt, kt = tiles(q), tiles(k)
    r8 = lax.broadcasted_iota(jnp.int32, (SUBLANES, HEAD_DIM), 0)
    row_masks = ((r8 & 1) == 1, (r8 & 2) == 2, r8 >= 4)
    in_tile = [_tile_levels(x, qt[j], kt[j], row_masks) for j, x in enumerate(tiles(logf))]
    levels = [(2 << b, jnp.concatenate([in_tile[j][1][b][0] for j in range(n_tiles)], axis=0),
               jnp.concatenate([in_tile[j][1][b][1] for j in range(n_tiles)], axis=0)) for b in range(3)]

    sc = [t[0] for t in in_tile]
    zero = jnp.zeros((SUBLANES, HEAD_DIM), F32)
    nb = 1
    while nb < n_tiles:
        ql = [zero] * n_tiles
        kl = [zero] * n_tiles
        new = list(sc)
        for m in range(n_tiles // (2 * nb)):
            lo = range(2 * m * nb, (2 * m + 1) * nb)
            up = range((2 * m + 1) * nb, (2 * m + 2) * nb)
            tot = _bc8(sc[lo[-1]][SUBLANES - 1:SUBLANES])
            for j in lo:
                kl[j] = kt[j] * jnp.exp2(tot - sc[j])
            for j in up:
                ql[j] = qt[j] * jnp.exp2(sc[j])
                new[j] = sc[j] + tot
        levels.append((2 * nb * SUBLANES, jnp.concatenate(ql, axis=0), jnp.concatenate(kl, axis=0)))
        sc = new
        nb *= 2
    a_t = sc
    a_last = a_t[n_tiles - 1][SUBLANES - 1:SUBLANES]
    qg = jnp.concatenate([qt[j] * jnp.exp2(a_t[j]) for j in range(n_tiles)], axis=0)
    kd = jnp.concatenate([kt[j] * jnp.exp2(_bc8(a_last) - a_t[j]) for j in range(n_tiles)], axis=0)
    diag = jnp.sum(q * k, axis=-1, keepdims=True)
    return ([(group, ql.astype(BF16), kl.astype(BF16)) for group, ql, kl in levels],
            diag, qg.astype(BF16), kd.astype(BF16), jnp.exp2(a_last))


def _hgrn_mix(sides, v, ga, gnorm, st):
    levels, diag, qg, kd, chunk_decay = sides
    row = lax.broadcasted_iota(jnp.int32, (HGRN_CHUNK, HGRN_CHUNK), 0)
    col = lax.broadcasted_iota(jnp.int32, (HGRN_CHUNK, HGRN_CHUNK), 1)
    differ = row ^ col
    scores = jnp.where(differ == 0, diag, 0.0)
    for group, ql, kl in levels:
        x = _dot_nt(ql, kl)
        if group < HGRN_CHUNK:
            x = jnp.where(differ < group, x, 0.0)
        scores = scores + x

    o = _dot_nt(qg, st.astype(BF16)) + _dot(scores.astype(BF16), v.astype(BF16))
    st_new = st * chunk_decay + _dot(v.T.astype(BF16), kd)

    ms = jnp.mean(o * o, axis=-1, keepdims=True)
    o = o * lax.rsqrt(ms + RMS_EPS) * gnorm * (ga * jax.nn.sigmoid(ga))
    return o, st_new


def _hgrn_prompt_kernel(q_ref, f_ref, i_ref, g_ref, lbp_ref, gn_ref, o_ref, s_ref, st_scr, *, n_chunks):
    t = pl.program_id(1)

    @pl.when(t == 0)
    def _():
        st_scr[...] = jnp.zeros(st_scr.shape, F32)

    gnorm = gn_ref[...]

    def chunk_body(c, carry):
        rows = pl.ds(pl.multiple_of(c * HGRN_CHUNK, HGRN_CHUNK), HGRN_CHUNK)

        def head_lanes(h):
            return slice(h * HEAD_DIM, (h + 1) * HEAD_DIM)

        def sides_of(h):
            lanes = head_lanes(h)
            return _hgrn_sides(q_ref[rows, lanes], f_ref[rows, lanes], _lower_bound(lbp_ref[:, lanes]))

        sides = sides_of(0)
        for h in range(N_HEADS):
            lanes = head_lanes(h)
            nxt = sides_of(h + 1) if h + 1 < N_HEADS else None
            o, st_new = _hgrn_mix(sides, i_ref[rows, lanes].astype(F32), g_ref[rows, lanes].astype(F32),
                                  gnorm, st_scr[h])
            st_scr[h] = st_new
            o_ref[rows, lanes] = o.astype(o_ref.dtype)
            sides = nxt
        return carry

    lax.fori_loop(0, n_chunks, chunk_body, 0)

    @pl.when(t == pl.num_programs(1) - 1)
    def _():
        for h in range(N_HEADS):
            s_ref[h] = st_scr[h].T


def _hgrn_prompt(wide, narrow, lb_param, g_norm, batch, seq, tt):
    n_t = seq // tt

    def sec(s):
        return pl.BlockSpec((None, tt, D_MODEL), lambda b, t, s=s: (s, b * n_t + t, 0))

    return pl.pallas_call(
        functools.partial(_hgrn_prompt_kernel, n_chunks=tt // HGRN_CHUNK),
        grid=(batch, n_t),
        in_specs=[sec(WIDE_Q), sec(WIDE_F), sec(NARROW_I), sec(NARROW_G),
                  pl.BlockSpec(lb_param.shape, lambda b, t: (0, 0)),
                  pl.BlockSpec(g_norm.shape, lambda b, t: (0, 0))],
        out_specs=[pl.BlockSpec((tt, D_MODEL), lambda b, t: (b * n_t + t, 0)),
                   pl.BlockSpec((None, N_HEADS, HEAD_DIM, HEAD_DIM), lambda b, t: (b, 0, 0, 0))],
        out_shape=[jax.ShapeDtypeStruct((batch * seq, D_MODEL), BF16),
                   jax.ShapeDtypeStruct((batch, N_HEADS, HEAD_DIM, HEAD_DIM), F32)],
        scratch_shapes=[pltpu.VMEM((N_HEADS, HEAD_DIM, HEAD_DIM), F32)],
        compiler_params=_params(2),
        name="hgrn_prompt",
    )(wide, wide, narrow, narrow, lb_param, g_norm)


def _hgrn_sample_kernel(q_ref, f_ref, i_ref, g_ref, lbp_ref, gn_ref, s0_ref, o_ref, s_ref):
    sb = q_ref.shape[0]
    lb = _lower_bound(lbp_ref[...])
    f, k = _gates_from_z(f_ref[...], lb)
    qa = q_ref[...]
    q = qa * jax.nn.sigmoid(qa)
    ga = g_ref[...]
    gate = ga * jax.nn.sigmoid(ga)
    v = i_ref[...]
    gnorm = gn_ref[...]
    qf = q * f
    qk = q * k
    for h in range(N_HEADS):
        hl = slice(h * HEAD_DIM, (h + 1) * HEAD_DIM)
        f_cols = f[:, hl].T
        k_cols = k[:, hl].T
        for b in range(sb):
            s0 = s0_ref[b, h]
            v_row = v[b:b + 1, hl]
            s_ref[b, h] = s0 * f_cols[:, b:b + 1] + k_cols[:, b:b + 1] * v_row
            o = _dot(qf[b:b + 1, hl].astype(BF16), s0.astype(BF16)) \
                + jnp.sum(qk[b:b + 1, hl], axis=-1, keepdims=True) * v_row
            ms = jnp.mean(o * o, axis=-1, keepdims=True)
            o_ref[b:b + 1, hl] = o * lax.rsqrt(ms + RMS_EPS) * gnorm * gate[b:b + 1, hl]


def _hgrn_sample(wide, narrow, lb_param, g_norm, state, sb):
    n = state.shape[0]

    def sec(s):
        return pl.BlockSpec((None, sb, D_MODEL), lambda i, s=s: (s, i, 0))

    st_spec = pl.BlockSpec((sb, N_HEADS, HEAD_DIM, HEAD_DIM), lambda i: (i, 0, 0, 0))
    return pl.pallas_call(
        _hgrn_sample_kernel,
        grid=(n // sb,),
        in_specs=[sec(WIDE_Q), sec(WIDE_F), sec(NARROW_I), sec(NARROW_G),
                  pl.BlockSpec(lb_param.shape, lambda i: (0, 0)),
                  pl.BlockSpec(g_norm.shape, lambda i: (0, 0)),
                  st_spec],
        out_specs=[pl.BlockSpec((sb, D_MODEL), lambda i: (i, 0)), st_spec],
        out_shape=[jax.ShapeDtypeStruct((n, D_MODEL), F32),
                   jax.ShapeDtypeStruct(state.shape, F32)],
        compiler_params=_params(1),
        name="hgrn_sample",
    )(wide, wide, narrow, narrow, lb_param, g_norm, state)


def _alibi_slopes():
    return np.asarray([2.0 ** (-8.0 * (h + 1) / N_HEADS) for h in range(N_HEADS)], np.float32)


def _topk_rank(g, idx, n):
    axis, count = n
    rank = jnp.zeros(g.shape, F32)
    for m in range(count):
        gm = lax.slice_in_dim(g, m, m + 1, axis=axis)
        rank = rank + ((gm > g) | ((gm == g) & (idx > m))).astype(F32)
    return rank


def _moba_prompt_kernel(slopes_ref, q_ref, k_ref, v_ref, o_ref,
                        kb, vt, qtb, kmean, bias, dist, *, n_blk):
    h = pl.program_id(1)
    blk = MOBA_BLOCK
    n_pairs = n_blk // 2
    slope2 = slopes_ref[h] * LOG2E
    scale2 = HEAD_DIM ** -0.5 * LOG2E

    def blk_rows(n):
        return slice(n * blk, (n + 1) * blk)

    def prep(n, carry):
        rows = blk_rows(n)
        kk = k_ref[rows, :]
        kb[rows, :] = kk.astype(BF16)
        kmean[pl.ds(n, 1), :] = jnp.sum(kk, axis=0, keepdims=True) * (1.0 / blk)
        vt[:, rows] = v_ref[rows, :].T.astype(BF16)
        return carry

    for n in range(n_blk):
        prep(n, 0)

    key = lax.broadcasted_iota(jnp.int32, (blk, blk), 0)
    qry = lax.broadcasted_iota(jnp.int32, (blk, blk), 1)
    d0 = slope2 * (qry - key).astype(F32)
    dist[0] = d0
    dist[1] = jnp.where(key <= qry, d0, -MASKED)

    def choose(qi, carry):
        rows = blk_rows(qi)
        q_t = q_ref[rows, :].astype(F32).T
        qtb[:, rows] = (q_t * scale2).astype(BF16)
        gate = jnp.dot(kmean[...], q_t, precision=lax.Precision.HIGHEST, preferred_element_type=F32)
        nio = lax.broadcasted_iota(jnp.int32, gate.shape, 0)
        past = nio < qi
        g = jnp.where(past, gate, -jnp.inf)
        attend = (past & (_topk_rank(g, nio, (0, n_blk)) < MOBA_TOPK)) | (nio == qi)
        bias[:, rows] = jnp.where(attend, -slope2 * ((qi - nio) * blk).astype(F32), MASKED)
        return carry

    for qi in range(n_blk):
        choose(qi, 0)

    steps = [(j, i) for j in range(n_pairs) for i in range(j + 1)]

    def pair(n):
        return slice(n * 2 * blk, (n + 1) * 2 * blk)

    def raw_scores(j, i):
        return _dot(kb[pair(i), :], qtb[:, pair(j)])

    state = {}

    def flush(j, i, p, alpha):
        m, l, acc = state[j]
        acc = acc * alpha + _dot(vt[:, pair(i)], p)
        state[j] = (m, l, acc)
        if i == j:
            o_ref[pair(j), :] = (acc * (1.0 / l)).T.astype(o_ref.dtype)
            del state[j]

    s_next = raw_scores(*steps[0])
    pending = None
    for n, (j, i) in enumerate(steps):
        s = s_next
        if n + 1 < len(steps):
            s_next = raw_scores(*steps[n + 1])
        if pending is not None:
            flush(*pending)
        if i == 0:
            state[j] = (jnp.full((1, 2 * blk), M_INIT, F32), jnp.zeros((1, 2 * blk), F32),
                        jnp.zeros((HEAD_DIM, 2 * blk), F32))
        m, l, acc = state[j]
        own = 1 if i == j else 0
        ms, ls, alphas, ps = [], [], [], []
        for t in range(2 * blk // STRIP):
            lanes = slice(t * STRIP, (t + 1) * STRIP)
            in_blk = slice(t * STRIP % blk, t * STRIP % blk + STRIP)
            first = t * STRIP < blk
            xa = s[:blk, lanes] - dist[own if first else 0, :, in_blk]
            xb = s[blk:, lanes] - dist[0 if first else own, :, in_blk]
            ba = bias[2 * i:2 * i + 1, j * 2 * blk + t * STRIP:j * 2 * blk + (t + 1) * STRIP]
            bb = bias[2 * i + 1:2 * i + 2, j * 2 * blk + t * STRIP:j * 2 * blk + (t + 1) * STRIP]
            m_t = jnp.maximum(m[:, lanes], jnp.maximum(jnp.max(xa, axis=0, keepdims=True) + ba,
                                                       jnp.max(xb, axis=0, keepdims=True) + bb))
            alpha_t = jnp.exp2(m[:, lanes] - m_t)
            pa = jnp.exp2(xa - (m_t - ba))
            pb = jnp.exp2(xb - (m_t - bb))
            ls.append(alpha_t * l[:, lanes] + jnp.sum(pa, axis=0, keepdims=True)
                      + jnp.sum(pb, axis=0, keepdims=True))
            ms.append(m_t)
            alphas.append(alpha_t)
            ps.append(jnp.concatenate([pa.astype(BF16), pb.astype(BF16)], axis=0))
        state[j] = (jnp.concatenate(ms, axis=1), jnp.concatenate(ls, axis=1), acc)
        pending = (j, i, jnp.concatenate(ps, axis=1), jnp.concatenate(alphas, axis=1))
    flush(*pending)


def _moba_prompt(narrow, k_rows, v_rows, batch, seq):
    n_blk = seq // MOBA_BLOCK
    assert n_blk % 2 == 0, "key blocks are processed in pairs"
    head = pl.BlockSpec((seq, HEAD_DIM), lambda b, h: (b, h))
    return pl.pallas_call(
        functools.partial(_moba_prompt_kernel, n_blk=n_blk),
        grid=(batch, N_HEADS),
        in_specs=[pl.BlockSpec(memory_space=pltpu.SMEM),
                  pl.BlockSpec((None, seq, HEAD_DIM), lambda b, h: (NARROW_MQ, b, h)),
                  head, head],
        out_specs=head,
        out_shape=jax.ShapeDtypeStruct((batch * seq, D_MODEL), BF16),
        scratch_shapes=[pltpu.VMEM((seq, HEAD_DIM), BF16),
                        pltpu.VMEM((HEAD_DIM, seq), BF16),
                        pltpu.VMEM((HEAD_DIM, seq), BF16),
                        pltpu.VMEM((n_blk, HEAD_DIM), F32),
                        pltpu.VMEM((n_blk, seq), F32),
                        pltpu.VMEM((2, MOBA_BLOCK, MOBA_BLOCK), F32)],
        compiler_params=_params(2),
        name="moba_prompt",
    )(jnp.asarray(_alibi_slopes()), narrow, k_rows, v_rows)


def _page_heads_on_lanes(page_ref):
    return jnp.concatenate([page_ref[pl.ds(h, PAGE_SIZE, stride=N_HEADS), :] for h in range(N_HEADS)], axis=1)


def _moba_sample_kernel(pt_ref, slopes_ref, q_ref, kn_ref, vn_ref, *rest, n_pages):
    del pt_ref
    k_refs, v_refs = rest[:n_pages], rest[n_pages:2 * n_pages]
    o_ref, sc, ksum, p_scr, acc0, linv = rest[2 * n_pages:]
    i = pl.program_id(0)
    n_seq = pl.num_programs(0) - 1
    past_len = n_pages * PAGE_SIZE
    n_blk = past_len // MOBA_BLOCK
    pages_per_blk = MOBA_BLOCK // PAGE_SIZE
    scale = HEAD_DIM ** -0.5

    @pl.when(i > 0)
    def _():
        slot = lax.rem(i - 1, 2)
        a = acc0[slot]
        for g in range(n_pages):
            pj = p_scr[slot, :, g * PAGE_SIZE:(g + 1) * PAGE_SIZE]
            a = a + _dot(pj.astype(BF16), _page_heads_on_lanes(v_refs[g]).astype(BF16))
        a = a * linv[slot][:, 0:1]
        for h in range(N_HEADS):
            hl = slice(h * HEAD_DIM, (h + 1) * HEAD_DIM)
            o_ref[:, hl] = a[h:h + 1, hl]

    @pl.when(i < n_seq)
    def _():
        slot = lax.rem(i, 2)
        q = q_ref[...].astype(F32)
        head_of_lane = lax.shift_right_logical(
            lax.broadcasted_iota(jnp.int32, (N_HEADS, D_MODEL), 1), HEAD_DIM.bit_length() - 1)
        head_of_row = lax.broadcasted_iota(jnp.int32, (N_HEADS, D_MODEL), 0)
        qr = jnp.where(head_of_lane == head_of_row, jnp.broadcast_to(q, (N_HEADS, D_MODEL)), 0.0)
        qrb = qr.astype(BF16)
        sums = []
        for g in range(n_pages):
            kp = _page_heads_on_lanes(k_refs[g])
            sc[:, g * PAGE_SIZE:(g + 1) * PAGE_SIZE] = _dot_nt(qrb, kp.astype(BF16))
            sums.append(jnp.sum(kp, axis=0, keepdims=True))
        for b in range(n_blk):
            tot = sums[b * pages_per_blk]
            for r in range(1, pages_per_blk):
                tot = tot + sums[b * pages_per_blk + r]
            ksum[b:b + 1, :] = tot

        kmean = ksum[...] * (1.0 / MOBA_BLOCK)
        gate = _dot_nt(qr, kmean, precision=lax.Precision.HIGHEST)
        nio = lax.broadcasted_iota(jnp.int32, gate.shape, 1)
        sel = _topk_rank(gate, nio, (1, n_blk)) < MOBA_TOPK
        kpos = lax.broadcasted_iota(jnp.int32, (N_HEADS, past_len), 1)
        blk_of_key = lax.shift_right_logical(kpos, MOBA_BLOCK.bit_length() - 1)
        chosen = jnp.zeros((N_HEADS, past_len), jnp.bool_)
        for n in range(n_blk):
            chosen = chosen | ((blk_of_key == n) & sel[:, n:n + 1])
        slope = slopes_ref[...]
        s_all = sc[...] * scale - slope * (past_len - kpos).astype(F32)
        s_all = jnp.where(chosen, s_all, MASKED)
        s_self = jnp.sum(qr * kn_ref[...], axis=-1, keepdims=True) * scale
        m = jnp.maximum(jnp.max(s_all, axis=-1, keepdims=True), s_self)
        p = jnp.exp(s_all - m)
        p_self = jnp.exp(s_self - m)
        p_scr[slot] = p
        linv[slot] = jnp.broadcast_to(1.0 / (jnp.sum(p, axis=-1, keepdims=True) + p_self), linv.shape[1:])
        acc0[slot] = p_self * jnp.broadcast_to(vn_ref[...], (N_HEADS, D_MODEL))


def _moba_sample(narrow3, k_new, v_new, cache_k, cache_v, page_table):
    n_seq, n_pages = page_table.shape
    past_len = n_pages * PAGE_SIZE
    assert past_len % MOBA_BLOCK == 0, "the decode token must start a MoBA block"
    n_blk = past_len // MOBA_BLOCK
    page_rows = PAGE_SIZE * N_HEADS

    def k_seq(i):
        return jnp.minimum(i, n_seq - 1)

    def v_seq(i):
        return jnp.maximum(i - 1, 0)

    new_row = pl.BlockSpec((None, 1, D_MODEL), lambda i, pt: (k_seq(i), 0, 0))

    def k_page(g):
        return pl.BlockSpec((page_rows, HEAD_DIM), lambda i, pt: (pt[k_seq(i) * n_pages + g], 0))

    def v_page(g):
        return pl.BlockSpec((page_rows, HEAD_DIM), lambda i, pt: (pt[v_seq(i) * n_pages + g], 0))

    grid_spec = pltpu.PrefetchScalarGridSpec(
        num_scalar_prefetch=1,
        grid=(n_seq + 1,),
        in_specs=[pl.BlockSpec((N_HEADS, 1), lambda i, pt: (0, 0)),
                  pl.BlockSpec((None, None, 1, D_MODEL), lambda i, pt: (NARROW_MQ, k_seq(i), 0, 0)), new_row, new_row]
                 + [k_page(g) for g in range(n_pages)] + [v_page(g) for g in range(n_pages)],
        out_specs=pl.BlockSpec((None, 1, D_MODEL), lambda i, pt: (v_seq(i), 0, 0)),
        scratch_shapes=[pltpu.VMEM((N_HEADS, past_len), F32),
                        pltpu.VMEM((n_blk, D_MODEL), F32),
                        pltpu.VMEM((2, N_HEADS, past_len), F32),
                        pltpu.VMEM((2, N_HEADS, D_MODEL), F32),
                        pltpu.VMEM((2, N_HEADS, HEAD_DIM), F32)])
    return pl.pallas_call(
        functools.partial(_moba_sample_kernel, n_pages=n_pages),
        grid_spec=grid_spec,
        out_shape=jax.ShapeDtypeStruct((n_seq, 1, D_MODEL), F32),
        compiler_params=_params(1),
        name="moba_sample",
    )(page_table.reshape(-1), jnp.asarray(_alibi_slopes()).reshape(N_HEADS, 1),
      narrow3, k_new, v_new, *([cache_k] * n_pages), *([cache_v] * n_pages))


def _layer_norm(x, g, b):
    mu = jnp.mean(x, axis=-1, keepdims=True)
    xc = x - mu
    var = jnp.mean(xc * xc, axis=-1, keepdims=True)
    return xc * lax.rsqrt(var + LN_EPS) * g + b


def _mix_kernel(oa_ref, ob_ref, ga_ref, gb_ref, x_ref, wa_ref, wb_ref, wo_ref, g_ref, b_ref, h_ref):
    ya = _dot(oa_ref[...].astype(BF16), wa_ref[...])
    yb = _dot(ob_ref[...].astype(BF16), wb_ref[...])
    mixed = jax.nn.sigmoid(ga_ref[...].astype(F32)) * ya + jax.nn.sigmoid(gb_ref[...].astype(F32)) * yb
    mix = _dot(mixed.astype(BF16), wo_ref[...])
    h_ref[...] = _layer_norm(DEEPNORM_ALPHA * x_ref[...] + mix, g_ref[...], b_ref[...])


def _mix(o_a, o_b, narrow, x2d, wa, wb, wo, ln_g, ln_b, tm):
    m = x2d.shape[0]
    rows = pl.BlockSpec((tm, D_MODEL), lambda i: (i, 0))
    sec = lambda s: pl.BlockSpec((None, tm, D_MODEL), lambda i, s=s: (s, i, 0))
    full = lambda a: pl.BlockSpec(a.shape, lambda i: (0, 0))
    return pl.pallas_call(
        _mix_kernel,
        grid=(m // tm,),
        in_specs=[rows, rows, sec(NARROW_GA), sec(NARROW_GB), rows, full(wa), full(wb), full(wo), full(ln_g), full(ln_b)],
        out_specs=rows,
        out_shape=jax.ShapeDtypeStruct((m, D_MODEL), F32),
        compiler_params=_params(1),
        name="mix",
    )(o_a, o_b, narrow, narrow, x2d, wa, wb, wo, ln_g, ln_b)


def _mlp_kernel(h_ref, wu_ref, wd_ref, g_ref, b_ref, y_ref, acc, hb):
    j = pl.program_id(1)

    @pl.when(j == 0)
    def _():
        hb[...] = h_ref[...].astype(BF16)
        acc[...] = jnp.zeros(acc.shape, F32)

    u = jnp.maximum(_dot(hb[...], wu_ref[...]), 0.0)
    acc[...] += _dot((u * u).astype(BF16), wd_ref[...])

    @pl.when(j == pl.num_programs(1) - 1)
    def _():
        y_ref[...] = _layer_norm(DEEPNORM_ALPHA * h_ref[...] + acc[...], g_ref[...], b_ref[...])


def _mlp(h, wu, wd, ln_g, ln_b, tm, tf):
    m = h.shape[0]
    d_ff = wu.shape[1]
    rows = pl.BlockSpec((tm, D_MODEL), lambda i, j: (i, 0))
    full = lambda a: pl.BlockSpec(a.shape, lambda i, j: (0, 0))
    return pl.pallas_call(
        _mlp_kernel,
        grid=(m // tm, d_ff // tf),
        in_specs=[rows,
                  pl.BlockSpec((D_MODEL, tf), lambda i, j: (0, j), pipeline_mode=pl.Buffered(1)),
                  pl.BlockSpec((tf, D_MODEL), lambda i, j: (j, 0), pipeline_mode=pl.Buffered(1)),
                  full(ln_g), full(ln_b)],
        out_specs=rows,
        out_shape=jax.ShapeDtypeStruct((m, D_MODEL), F32),
        scratch_shapes=[pltpu.VMEM((tm, D_MODEL), F32), pltpu.VMEM((tm, D_MODEL), BF16)],
        compiler_params=_params(2),
        name="mlp",
    )(h, wu, wd, ln_g, ln_b)


def kernel(x_prompt, x_sample, cache_k, cache_v, state_hgrn, page_table, w_in, lb_param, g_norm_a,
           w_proj_a, w_proj_b, w_out, ln1_g, ln1_b, w_up, w_down, ln2_g, ln2_b):
    assert w_in.shape[0] == DEPTH
    batch, seq, _ = x_prompt.shape
    n_seq, dec_seq, _ = x_sample.shape
    assert dec_seq == 1
    n_pool = cache_k.shape[1]

    w_in_b = w_in[0].astype(BF16)
    wa, wb, wo = w_proj_a[0].astype(BF16), w_proj_b[0].astype(BF16), w_out[0].astype(BF16)
    wu, wd = w_up[0].astype(BF16), w_down[0].astype(BF16)
    gn = g_norm_a[0].reshape(1, HEAD_DIM)
    g1, b1 = ln1_g[0].reshape(1, D_MODEL), ln1_b[0].reshape(1, D_MODEL)
    g2, b2 = ln2_g[0].reshape(1, D_MODEL), ln2_b[0].reshape(1, D_MODEL)

    xp = x_prompt.reshape(batch * seq, D_MODEL)
    wide_p, narrow_p, k_p, v_p = _inproj(xp, w_in_b, tm=1024)
    oa_p, st_p = _hgrn_prompt(wide_p, narrow_p, lb_param, gn, batch, seq, tt=512)
    ob_p = _moba_prompt(narrow_p, k_p, v_p, batch, seq)
    h_p = _mix(oa_p, ob_p, narrow_p, xp, wa, wb, wo, g1, b1, tm=512)
    y_p = _mlp(h_p, wu, wd, g2, b2, tm=512, tf=4096)

    xs = x_sample.reshape(n_seq, D_MODEL)
    wide_s, narrow_s, k_s, v_s = _inproj(xs, w_in_b, tm=n_seq)
    narrow_s = narrow_s.astype(F32)
    oa_s, st_s = _hgrn_sample(wide_s, narrow_s, lb_param, gn, state_hgrn[0], sb=8)
    ob_s = _moba_sample(narrow_s.reshape(N_NARROW, n_seq, 1, D_MODEL),
                        k_s.reshape(n_seq, 1, D_MODEL), v_s.reshape(n_seq, 1, D_MODEL),
                        cache_k[0].reshape(n_pool * PAGE_SIZE * N_HEADS, HEAD_DIM),
                        cache_v[0].reshape(n_pool * PAGE_SIZE * N_HEADS, HEAD_DIM), page_table)
    h_s = _mix(oa_s, ob_s.reshape(n_seq, D_MODEL), narrow_s, xs, wa, wb, wo, g1, b1, tm=n_seq)
    y_s = _mlp(h_s, wu, wd, g2, b2, tm=n_seq, tf=1024)

    kv_p = (DEPTH, batch, seq, N_HEADS, HEAD_DIM)
    kv_s = (DEPTH, n_seq, dec_seq, N_HEADS, HEAD_DIM)
    return (y_p.reshape(batch, seq, D_MODEL), y_s.reshape(n_seq, dec_seq, D_MODEL),
            k_p.reshape(kv_p), v_p.reshape(kv_p), st_p[None],
            k_s.reshape(kv_s), v_s.reshape(kv_s), st_s[None])
```
